```python
import functools
import jax, jax.numpy as jnp
from jax import lax
import numpy as np

D_MODEL = 1024
BATCH = 8
SEQ = 2048
DEPTH = 2
DEC_BATCH = 32
DEC_SEQ = 4
PAST_LEN = 16384
PAGE_SIZE = 128

A_WIDTH = D_MODEL // 2
A_GROUPS = 4
A_GROUP_DIM = A_WIDTH // A_GROUPS
A_CHUNK = 128
B_WIDTH = D_MODEL // 2
B_WINDOWS = (2, 4, 8, 16)
B_GROUP_DIM = B_WIDTH // len(B_WINDOWS)
B_STATE = max(B_WINDOWS) - 1
C_HEADS = 8
C_HEAD_DIM = 64
C_WIDTH = C_HEADS * C_HEAD_DIM
C_BLOCK = 256
C_TOP_K = 3
Q_BLOCK = 32
ROPE_DIM = C_HEAD_DIM // 4
ROPE_THETA = 500000.0
N_BRANCH = 3
D_FF = 4 * D_MODEL
EPS = 1e-6
IN_SPLITS = (A_WIDTH, 2 * A_WIDTH, 2 * A_WIDTH + B_WIDTH, 2 * A_WIDTH + B_WIDTH + C_WIDTH,
             2 * A_WIDTH + B_WIDTH + 2 * C_WIDTH, 2 * A_WIDTH + B_WIDTH + 3 * C_WIDTH)
IN_WIDTH = 2 * A_WIDTH + B_WIDTH + 3 * C_WIDTH + N_BRANCH * D_MODEL

kernel_name = 'hybrid_gmlp_pool_moba_decode_step'


def rmsnorm(x, g):
    xf = x.astype(jnp.float32)
    y = xf * lax.rsqrt(jnp.mean(xf * xf, axis=-1, keepdims=True) + EPS)
    return (y * g.astype(jnp.float32)).astype(x.dtype)


def rope_partial(x, pos):
    half = ROPE_DIM // 2
    inv = 1.0 / (ROPE_THETA ** (jnp.arange(half, dtype=jnp.float32) / half))
    ang = pos.astype(jnp.float32)[:, None] * inv[None, :]
    cos = jnp.cos(ang)[None, :, None, :]
    sin = jnp.sin(ang)[None, :, None, :]
    xr = x[..., :ROPE_DIM].astype(jnp.float32)
    x1, x2 = xr[..., :half], xr[..., half:]
    rot = jnp.concatenate([x1 * cos - x2 * sin, x1 * sin + x2 * cos], axis=-1).astype(x.dtype)
    return jnp.concatenate([rot, x[..., ROPE_DIM:]], axis=-1)


def chunk_gmlp(u, vn, w_s, b_s):
    B_, S_, _ = vn.shape
    n_chunks = -(-S_ // A_CHUNK)
    pad = n_chunks * A_CHUNK - S_
    vp = jnp.pad(vn, ((0, 0), (0, pad), (0, 0))).reshape(B_, n_chunks, A_CHUNK, A_GROUPS, A_GROUP_DIM)
    causal = jnp.tril(jnp.ones((A_CHUNK, A_CHUNK), dtype=bool))
    w = jnp.where(causal[None], w_s, 0).astype(vn.dtype)
    mixed = jnp.einsum('gts,bnsgc->bntgc', w, vp) + b_s.T[None, None, :, :, None]
    mixed = mixed.reshape(B_, n_chunks * A_CHUNK, A_WIDTH)[:, :S_]
    return u * mixed


def multiscale_pool(xb, prefix, pos0, w_pool, pool_scale):
    B_, S_, _ = xb.shape
    xp_raw = jnp.concatenate([prefix.astype(xb.dtype), xb], axis=1)
    xp = xp_raw.astype(jnp.float32)
    cs = jnp.concatenate([jnp.zeros((B_, 1, B_WIDTH), jnp.float32), jnp.cumsum(xp, axis=1)], axis=1)
    pos = pos0 + jnp.arange(S_)
    end = cs[:, B_STATE + 1:]
    means = []
    for gi, w in enumerate(B_WINDOWS):
        sl = slice(gi * B_GROUP_DIM, (gi + 1) * B_GROUP_DIM)
        start = cs[:, B_STATE + 1 - w: B_STATE + 1 - w + S_, sl]
        cnt = jnp.minimum(w, pos + 1).astype(jnp.float32)[None, :, None]
        means.append((end[..., sl] - start) / cnt)
    pooled = (jnp.concatenate(means, axis=-1) - xp[:, B_STATE:]).astype(xb.dtype)
    pooled = pooled.reshape(B_, S_, len(B_WINDOWS), B_GROUP_DIM)
    mixed = jnp.einsum('bsgc,gcd->bsgd', pooled, w_pool).reshape(B_, S_, B_WIDTH)
    return mixed * pool_scale, xp_raw[:, -B_STATE:]


def gather_blocks(blocks, idx):
    return jax.vmap(jax.vmap(lambda bl, ix: bl[ix]))(blocks, idx)


def block_attend(q, k_own, v_own, own_mask, k_sel=None, v_sel=None, sel_ok=None):
    f32 = jnp.float32
    scale = C_HEAD_DIM ** -0.5
    lo = jnp.einsum('bhqd,bhkd->bhqk', q, k_own, preferred_element_type=f32) * scale
    lo = jnp.where(own_mask, lo, -jnp.inf)
    if k_sel is None:
        p = jax.nn.softmax(lo, axis=-1).astype(v_own.dtype)
        return jnp.einsum('bhqk,bhkd->bhqd', p, v_own, preferred_element_type=f32).astype(q.dtype)
    ls = jnp.einsum('bhqd,bhqrkd->bhqrk', q, k_sel, preferred_element_type=f32) * scale
    ls = jnp.where(sel_ok[:, :, None], ls, -jnp.inf)
    B_, H, Q, R, BS = ls.shape
    p = jax.nn.softmax(jnp.concatenate([ls.reshape(B_, H, Q, R * BS), lo], axis=-1), axis=-1)
    ps = p[..., :R * BS].reshape(B_, H, Q, R, BS).astype(v_sel.dtype)
    po = p[..., R * BS:].astype(v_own.dtype)
    out = (jnp.einsum('bhqrk,bhqrkd->bhqd', ps, v_sel, preferred_element_type=f32)
           + jnp.einsum('bhqk,bhkd->bhqd', po, v_own, preferred_element_type=f32))
    return out.astype(q.dtype)


def moba_prompt(q, k, v):
    B_, S_, H, Dh = q.shape
    nb = -(-S_ // C_BLOCK)
    pad = nb * C_BLOCK - S_
    padw = ((0, 0), (0, 0), (0, pad), (0, 0))
    qt = q.transpose(0, 2, 1, 3)
    kt = jnp.pad(k.transpose(0, 2, 1, 3), padw)
    vt = jnp.pad(v.transpose(0, 2, 1, 3), padw)
    kb = kt.reshape(B_, H, nb, C_BLOCK, Dh)
    vb = vt.reshape(B_, H, nb, C_BLOCK, Dh)
    kmean = jnp.mean(kb, axis=3, dtype=jnp.float32)
    n_sel = min(C_TOP_K, nb - 1)
    n_qb = S_ // Q_BLOCK
    qc = qt.reshape(B_, H, n_qb, Q_BLOCK, Dh).transpose(2, 0, 1, 3, 4)

    def step(args):
        q_blk, ci = args
        qpos = ci * Q_BLOCK + jnp.arange(Q_BLOCK)
        own0 = (ci * Q_BLOCK // C_BLOCK) * C_BLOCK
        k_own = lax.dynamic_slice_in_dim(kt, own0, C_BLOCK, axis=2)
        v_own = lax.dynamic_slice_in_dim(vt, own0, C_BLOCK, axis=2)
        own_mask = (own0 + jnp.arange(C_BLOCK))[None, :] <= qpos[:, None]
        if n_sel == 0:
            return block_attend(q_blk, k_own, v_own, own_mask)
        n_past = qpos // C_BLOCK
        gate = jnp.einsum('bhqd,bhjd->bhqj', q_blk.astype(jnp.float32), kmean)
        gate = jnp.where(jnp.arange(nb)[None, :] < n_past[:, None], gate, -jnp.inf)
        _, idx = lax.top_k(gate, n_sel)
        sel_ok = jnp.arange(n_sel)[None, :] < n_past[:, None]
        return block_attend(q_blk, k_own, v_own, own_mask,
                            gather_blocks(kb, idx), gather_blocks(vb, idx), sel_ok)

    out = lax.map(step, (qc, jnp.arange(n_qb)))
    return out.transpose(1, 0, 3, 2, 4).reshape(B_, S_, H, Dh)


def page_coords(page_table, pos):
    phys = jax.vmap(lambda row, p: row[p // PAGE_SIZE])(page_table, pos)
    return phys, pos % PAGE_SIZE


def moba_sample(q, k_new, v_new, cache_k, cache_v, layer, page_table):
    DB, DS, H, Dh = q.shape
    n_full = PAST_LEN // C_BLOCK
    own_start = n_full * C_BLOCK
    n_rem = PAST_LEN - own_start
    qt = q.transpose(0, 2, 1, 3)
    qpos = PAST_LEN + jnp.arange(DS)
    rem_pos = jnp.broadcast_to(own_start + jnp.arange(n_rem), (DB, n_rem))
    pg, off = page_coords(page_table, rem_pos)
    k_own = jnp.concatenate([cache_k[layer, pg, off].astype(k_new.dtype), k_new], axis=1).transpose(0, 2, 1, 3)
    v_own = jnp.concatenate([cache_v[layer, pg, off].astype(v_new.dtype), v_new], axis=1).transpose(0, 2, 1, 3)
    own_mask = (own_start + jnp.arange(n_rem + DS))[None, :] <= qpos[:, None]
    n_sel = min(C_TOP_K, n_full)
    if n_sel == 0:
        out = block_attend(qt, k_own, v_own, own_mask)
    else:
        n_pg = -(-(n_full * C_BLOCK) // PAGE_SIZE)
        k_past = cache_k[layer, page_table[:, :n_pg]].reshape(DB, n_pg * PAGE_SIZE, H, Dh)[:, :n_full * C_BLOCK]
        kmean = jnp.mean(k_past.reshape(DB, n_full, C_BLOCK, H, Dh), axis=2, dtype=jnp.float32)
        gate = jnp.einsum('bhqd,bjhd->bhqj', qt.astype(jnp.float32), kmean)
        _, idx = lax.top_k(gate, n_sel)
        sel_pos = idx[..., None] * C_BLOCK + jnp.arange(C_BLOCK)
        pg, off = page_coords(page_table, sel_pos)
        hidx = jnp.arange(H)[None, :, None, None, None]
        k_sel = cache_k[layer, pg, off, hidx].astype(q.dtype)
        v_sel = cache_v[layer, pg, off, hidx].astype(q.dtype)
        sel_ok = jnp.ones((DS, n_sel), dtype=bool)
        out = block_attend(qt, k_own, v_own, own_mask, k_sel, v_sel, sel_ok)
    return out.transpose(0, 2, 1, 3)


def trunk_layer(x, pos0, pool_prefix, attend, g_mix, w_in, g_av, w_s, b_s, w_pool, pool_scale,
                w_br_a, w_br_b, w_br_c, w_out, g_ffn, w_ff1, w_ff2):
    B_, S_, _ = x.shape
    h = rmsnorm(x, g_mix)
    z = h @ w_in
    u, va, xb, q, k, v, gates = jnp.split(z, IN_SPLITS, axis=-1)
    u = jax.nn.gelu(u)
    vn = rmsnorm(jax.nn.gelu(va), g_av)
    a_out = chunk_gmlp(u, vn, w_s, b_s)
    b_out, pool_state = multiscale_pool(xb, pool_prefix, pos0, w_pool, pool_scale)
    pos = pos0 + jnp.arange(S_)
    q = rope_partial(q.reshape(B_, S_, C_HEADS, C_HEAD_DIM), pos)
    k = rope_partial(k.reshape(B_, S_, C_HEADS, C_HEAD_DIM), pos)
    v = v.reshape(B_, S_, C_HEADS, C_HEAD_DIM)
    c_out = attend(q, k, v).reshape(B_, S_, C_WIDTH)
    ga, gb, gc = jnp.split(jax.nn.sigmoid(gates.astype(jnp.float32)).astype(x.dtype), N_BRANCH, axis=-1)
    merged = ga * (a_out @ w_br_a) + gb * (b_out @ w_br_b) + gc * (c_out @ w_br_c)
    x = x + merged @ w_out
    h2 = rmsnorm(x, g_ffn)
    x = x + jnp.square(jax.nn.relu(h2 @ w_ff1)) @ w_ff2
    return x, vn, pool_state, k, v


def setup_inputs(seed: int = 0) -> dict:
    key = jax.random.key(seed)
    ks = jax.random.split(key, 24)
    f32 = jnp.float32
    n_pages = PAST_LEN // PAGE_SIZE
    n_used = DEC_BATCH * n_pages
    n_phys = n_used + max(1, n_used // 4)

    def nrm(k, shape, scale):
        return jax.random.normal(k, shape, f32) * scale

    page_table = jax.random.permutation(ks[5], n_phys)[:n_used].reshape(DEC_BATCH, n_pages).astype(jnp.int32)
    return {
        'x_prompt': nrm(ks[0], (BATCH, SEQ, D_MODEL), 1.0),
        'x_sample': nrm(ks[1], (DEC_BATCH, DEC_SEQ, D_MODEL), 1.0),
        'state_pool': nrm(ks[2], (DEPTH, DEC_BATCH, B_STATE, B_WIDTH), 1.0),
        'cache_k': nrm(ks[3], (DEPTH, n_phys, PAGE_SIZE, C_HEADS, C_HEAD_DIM), 1.0),
        'cache_v': nrm(ks[4], (DEPTH, n_phys, PAGE_SIZE, C_HEADS, C_HEAD_DIM), 1.0),
        'page_table': page_table,
        'g_mix': 1.0 + nrm(ks[6], (DEPTH, D_MODEL), 0.02),
        'w_in': nrm(ks[7], (DEPTH, D_MODEL, IN_WIDTH), D_MODEL ** -0.5),
        'g_av': 1.0 + nrm(ks[8], (DEPTH, A_WIDTH), 0.02),
        'w_spatial': nrm(ks[9], (DEPTH, A_GROUPS, A_CHUNK, A_CHUNK), A_CHUNK ** -0.5),
        'b_spatial': 1.0 + nrm(ks[10], (DEPTH, A_GROUPS, A_CHUNK), 0.1),
        'w_pool': nrm(ks[11], (DEPTH, len(B_WINDOWS), B_GROUP_DIM, B_GROUP_DIM), B_GROUP_DIM ** -0.5),
        'pool_scale': 1.0 + nrm(ks[12], (DEPTH, B_WIDTH), 0.1),
        'w_br_a': nrm(ks[13], (DEPTH, A_WIDTH, D_MODEL), A_WIDTH ** -0.5),
        'w_br_b': nrm(ks[14], (DEPTH, B_WIDTH, D_MODEL), B_WIDTH ** -0.5),
        'w_br_c': nrm(ks[15], (DEPTH, C_WIDTH, D_MODEL), C_WIDTH ** -0.5),
        'w_out': nrm(ks[16], (DEPTH, D_MODEL, D_MODEL), D_MODEL ** -0.5),
        'g_ffn': 1.0 + nrm(ks[17], (DEPTH, D_MODEL), 0.02),
        'w_ff1': nrm(ks[18], (DEPTH, D_MODEL, D_FF), D_MODEL ** -0.5),
        'w_ff2': nrm(ks[19], (DEPTH, D_FF, D_MODEL), D_FF ** -0.5),
        'g_final': 1.0 + nrm(ks[20], (D_MODEL,), 0.02),
    }


def reference(x_prompt, x_sample, state_pool, cache_k, cache_v, page_table, g_mix, w_in, g_av,
              w_spatial, b_spatial, w_pool, pool_scale, w_br_a, w_br_b, w_br_c, w_out, g_ffn,
              w_ff1, w_ff2, g_final):
    xp, xs = x_prompt, x_sample
    kp_l, vp_l, ks_l, vs_l, pp_l, ps_l, av_l = [], [], [], [], [], [], []
    for l in range(DEPTH):
        wl = (g_mix[l], w_in[l], g_av[l], w_spatial[l], b_spatial[l], w_pool[l], pool_scale[l],
              w_br_a[l], w_br_b[l], w_br_c[l], w_out[l], g_ffn[l], w_ff1[l], w_ff2[l])
        B_, S_, _ = xp.shape
        zero_prefix = jnp.zeros((B_, B_STATE, B_WIDTH), xp.dtype)
        xp, _, pool_p, k_p, v_p = trunk_layer(xp, 0, zero_prefix, moba_prompt, *wl)
        attend_s = functools.partial(moba_sample, cache_k=cache_k, cache_v=cache_v, layer=l,
                                     page_table=page_table)
        xs, av_s, pool_s, k_s, v_s = trunk_layer(xs, PAST_LEN, state_pool[l], attend_s, *wl)
        kp_l.append(k_p.reshape(B_, S_ // PAGE_SIZE, PAGE_SIZE, C_HEADS, C_HEAD_DIM))
        vp_l.append(v_p.reshape(B_, S_ // PAGE_SIZE, PAGE_SIZE, C_HEADS, C_HEAD_DIM))
        ks_l.append(k_s)
        vs_l.append(v_s)
        pp_l.append(pool_p)
        ps_l.append(pool_s)
        av_l.append(av_s)
    y_prompt = rmsnorm(xp, g_final)
    y_sample = rmsnorm(xs, g_final)
    return (y_prompt, y_sample, jnp.stack(kp_l), jnp.stack(vp_l), jnp.stack(ks_l), jnp.stack(vs_l),
            jnp.stack(pp_l), jnp.stack(ps_l), jnp.stack(av_l))
```

```python
import functools

import jax
import jax.numpy as jnp
from jax import lax
from jax.experimental import pallas as pl
from jax.experimental.pallas import tpu as pltpu

D_MODEL = 1024
A_WIDTH = 512
A_GROUPS = 4
A_CHUNK = 128
B_WIDTH = 512
B_WINDOWS = (2, 4, 8, 16)
B_GROUP_DIM = 128
B_STATE = 15
C_HEADS = 8
C_HEAD_DIM = 64
C_WIDTH = 512
C_BLOCK = 256
C_TOP_K = 3
ROPE_DIM = 16
ROPE_THETA = 500000.0
D_FF = 4096
EPS = 1e-6
PAGE_SIZE = 128
PAGES_PER_BLOCK = C_BLOCK // PAGE_SIZE
POOL_STEPS = B_STATE + 1
LANES = 128
VMEM_LIMIT = 56 * 1024 * 1024

_F32 = jnp.float32
_BF16 = jnp.bfloat16
_NT = (((1,), (1,)), ((), ()))


def _rms(x, g):
    return x * lax.rsqrt(jnp.mean(x * x, axis=-1, keepdims=True) + EPS) * g


def _mix_in_body(x_ref, gmix_ref, win_ref, gav_ref, wsp_ref, bsp_ref, prefix_ref, wpool_ref,
                 pscale_ref, wbra_ref, wbrb_ref, cos_ref, sa_ref, sb_ref,
                 mab_ref, gc_ref, q_ref, k_ref, v_ref, vn_ref, tail_ref, ext_ref,
                 *, tm, stride, pos0, carry):
    hdr = POOL_STEPS * stride
    i = pl.program_id(1)
    x = x_ref[0]
    hb = _rms(x, gmix_ref[0]).astype(_BF16)

    def seg(lo, hi):
        return jnp.dot(hb, win_ref[0, :, lo:hi], preferred_element_type=_F32)

    u = jax.nn.gelu(seg(0, A_WIDTH), approximate=True)
    vn = _rms(jax.nn.gelu(seg(A_WIDTH, 2 * A_WIDTH), approximate=True), gav_ref[0])
    vn_ref[0] = vn
    vnb = vn.astype(_BF16)
    chunks = []
    for c in range(tm // A_CHUNK):
        rows = slice(c * A_CHUNK, (c + 1) * A_CHUNK)
        groups = [jnp.dot(wsp_ref[0, g], vnb[rows, g * LANES:(g + 1) * LANES],
                          preferred_element_type=_F32) for g in range(A_GROUPS)]
        chunks.append(jnp.concatenate(groups, axis=1))
    mixed = chunks[0] if len(chunks) == 1 else jnp.concatenate(chunks, axis=0)
    a_out = u * (mixed + bsp_ref[...])

    xb = seg(2 * A_WIDTH, 2 * A_WIDTH + B_WIDTH)
    if carry:
        @pl.when(i == 0)
        def _():
            ext_ref[0:hdr, :] = prefix_ref[0]

        @pl.when(i > 0)
        def _():
            ext_ref[0:hdr, :] = ext_ref[tm:tm + hdr, :]
    else:
        ext_ref[0:hdr, :] = prefix_ref[0]
    ext_ref[hdr:hdr + tm, :] = xb
    tail_ref[0, 0] = ext_ref[tm:tm + hdr, :]
    row = lax.broadcasted_iota(jnp.int32, (tm, 1), 0)
    pos = pos0 + lax.shift_right_logical(i * tm + row, stride.bit_length() - 1)
    pooled = []
    for gi, w in enumerate(B_WINDOWS):
        cols = slice(gi * B_GROUP_DIM, (gi + 1) * B_GROUP_DIM)
        cur = xb[:, cols]
        s = cur
        for kk in range(1, w):
            s = s + ext_ref[hdr - kk * stride:hdr - kk * stride + tm, cols]
        cnt = jnp.minimum(w, pos + 1).astype(_F32)
        pg = (s / cnt - cur).astype(_BF16)
        pooled.append(jnp.dot(pg, wpool_ref[0, gi], preferred_element_type=_F32))
    b_out = jnp.concatenate(pooled, axis=1) * pscale_ref[0]

    g0 = 2 * A_WIDTH + B_WIDTH + 3 * C_WIDTH
    ga = jax.nn.sigmoid(seg(g0, g0 + D_MODEL))
    gb = jax.nn.sigmoid(seg(g0 + D_MODEL, g0 + 2 * D_MODEL))
    gc_ref[0] = jax.nn.sigmoid(seg(g0 + 2 * D_MODEL, g0 + 3 * D_MODEL))
    mab_ref[0] = (ga * jnp.dot(a_out.astype(_BF16), wbra_ref[0], preferred_element_type=_F32)
                  + gb * jnp.dot(b_out.astype(_BF16), wbrb_ref[0], preferred_element_type=_F32))

    c0 = 2 * A_WIDTH + B_WIDTH
    cosv, sav, sbv = cos_ref[...], sa_ref[...], sb_ref[...]

    def rope(z):
        parts = []
        for cb in range(C_WIDTH // LANES):
            zc = z[:, cb * LANES:(cb + 1) * LANES]
            parts.append(zc * cosv
                         + pltpu.roll(zc, LANES - ROPE_DIM // 2, axis=1) * sav
                         + pltpu.roll(zc, ROPE_DIM // 2, axis=1) * sbv)
        return jnp.concatenate(parts, axis=1)

    q_ref[0] = rope(seg(c0, c0 + C_WIDTH))
    k_ref[0] = rope(seg(c0 + C_WIDTH, c0 + 2 * C_WIDTH))
    v_ref[0] = seg(c0 + 2 * C_WIDTH, c0 + 3 * C_WIDTH)


def _const_spec(shape, layer):
    nd = len(shape)
    return pl.BlockSpec((1,) + tuple(shape[1:]), lambda *_: (layer,) + (0,) * (nd - 1),
                        pipeline_mode=pl.Buffered(1))


def _mix_in(x, layer, p, wsp, bsp, prefix, cos_t, sa_t, sb_t, *, tm, stride, pos0):
    nb, s, _ = x.shape
    nt = s // tm
    hdr = POOL_STEPS * stride
    row3 = lambda width: pl.BlockSpec((1, tm, width), lambda b, i: (b, i, 0))
    tab = pl.BlockSpec((tm, LANES), lambda b, i: (i, 0))
    in_specs = [
        row3(D_MODEL),
        _const_spec(p['g_mix'].shape, layer),
        _const_spec(p['w_in'].shape, layer),
        _const_spec(p['g_av'].shape, layer),
        _const_spec(wsp.shape, layer),
        pl.BlockSpec((tm, A_WIDTH), lambda b, i: (0, 0), pipeline_mode=pl.Buffered(1)),
        pl.BlockSpec((1, hdr, B_WIDTH), lambda b, i: (0, 0, 0), pipeline_mode=pl.Buffered(1)),
        _const_spec(p['w_pool'].shape, layer),
        _const_spec(p['pool_scale'].shape, layer),
        _const_spec(p['w_br_a'].shape, layer),
        _const_spec(p['w_br_b'].shape, layer),
        tab, tab, tab,
    ]
    out_specs = [row3(D_MODEL), row3(D_MODEL), row3(C_WIDTH), row3(C_WIDTH), row3(C_WIDTH),
                 row3(A_WIDTH), pl.BlockSpec((1, 1, hdr, B_WIDTH), lambda b, i: (b, i, 0, 0))]
    f = lambda width: jax.ShapeDtypeStruct((nb, s, width), _F32)
    out_shape = [f(D_MODEL), f(D_MODEL), f(C_WIDTH), f(C_WIDTH), f(C_WIDTH), f(A_WIDTH),
                 jax.ShapeDtypeStruct((nb, nt, hdr, B_WIDTH), _F32)]
    body = functools.partial(_mix_in_body, tm=tm, stride=stride, pos0=pos0, carry=nt > 1)
    return pl.pallas_call(
        body, grid=(nb, nt), in_specs=in_specs, out_specs=out_specs, out_shape=out_shape,
        scratch_shapes=[pltpu.VMEM((hdr + tm, B_WIDTH), _F32)],
        compiler_params=pltpu.CompilerParams(
            dimension_semantics=("arbitrary", "arbitrary"), vmem_limit_bytes=VMEM_LIMIT),
        name="mix_in",
    )(x, p['g_mix'], p['w_in'], p['g_av'], wsp, bsp, prefix, p['w_pool'], p['pool_scale'],
      p['w_br_a'], p['w_br_b'], cos_t, sa_t, sb_t)


def _mix_out_body(mab_ref, gc_ref, c_ref, x_ref, wbrc_ref, wout_ref, gffn_ref, wff1_ref, wff2_ref,
                  gfin_ref, o_ref, *, final, ff_chunk):
    merged = mab_ref[0] + gc_ref[0] * jnp.dot(c_ref[0].astype(_BF16), wbrc_ref[0],
                                              preferred_element_type=_F32)
    x1 = x_ref[0] + jnp.dot(merged.astype(_BF16), wout_ref[0], preferred_element_type=_F32)
    h2 = _rms(x1, gffn_ref[0]).astype(_BF16)
    ff = None
    for c in range(D_FF // ff_chunk):
        cols = slice(c * ff_chunk, (c + 1) * ff_chunk)
        f = jnp.dot(h2, wff1_ref[0, :, cols], preferred_element_type=_F32)
        f = jnp.square(jnp.maximum(f, 0.0)).astype(_BF16)
        part = jnp.dot(f, wff2_ref[0, cols, :], preferred_element_type=_F32)
        ff = part if ff is None else ff + part
    x2 = x1 + ff
    o_ref[0] = _rms(x2, gfin_ref[...]) if final else x2


def _mix_out(mab, gc, c, x, layer, p, g_final, *, tm, final):
    nb, s, _ = x.shape
    row3 = lambda width: pl.BlockSpec((1, tm, width), lambda b, i: (b, i, 0))
    in_specs = [
        row3(D_MODEL), row3(D_MODEL), row3(C_WIDTH), row3(D_MODEL),
        _const_spec(p['w_br_c'].shape, layer),
        _const_spec(p['w_out'].shape, layer),
        _const_spec(p['g_ffn'].shape, layer),
        _const_spec(p['w_ff1'].shape, layer),
        _const_spec(p['w_ff2'].shape, layer),
        pl.BlockSpec((1, D_MODEL), lambda b, i: (0, 0), pipeline_mode=pl.Buffered(1)),
    ]
    body = functools.partial(_mix_out_body, final=final, ff_chunk=1024)
    return pl.pallas_call(
        body, grid=(nb, s // tm), in_specs=in_specs, out_specs=row3(D_MODEL),
        out_shape=jax.ShapeDtypeStruct((nb, s, D_MODEL), _F32),
        compiler_params=pltpu.CompilerParams(
            dimension_semantics=("arbitrary", "arbitrary"), vmem_limit_bytes=VMEM_LIMIT),
        name="mix_out",
    )(mab, gc, c, x, p['w_br_c'], p['w_out'], p['g_ffn'], p['w_ff1'], p['w_ff2'], g_final)


def _prompt_attn_body(q_ref, k_ref, v_ref, o_ref, kb_ref, vt_ref, kmean_ref, bias_ref, *, nblk):
    qi = pl.program_id(2)
    scale = C_HEAD_DIM ** -0.5

    @pl.when(qi == 0)
    def _():
        for j in range(nblk):
            rows = slice(j * C_BLOCK, (j + 1) * C_BLOCK)
            kf = k_ref[0, rows, :]
            kb_ref[j] = kf.astype(_BF16)
            kmean_ref[j:j + 1, :] = jnp.sum(kf, axis=0, keepdims=True) * (1.0 / C_BLOCK)
            vt_ref[j] = v_ref[0, rows, :].T.astype(_BF16)

    qf = q_ref[0]
    lane = lax.broadcasted_iota(jnp.int32, qf.shape, 1)
    blk = lax.broadcasted_iota(jnp.int32, (nblk, C_BLOCK), 0)
    key_i = lax.broadcasted_iota(jnp.int32, (C_BLOCK, C_BLOCK), 0)
    qry_i = lax.broadcasted_iota(jnp.int32, (C_BLOCK, C_BLOCK), 1)
    neg = jnp.float32(-jnp.inf)
    heads = []
    for hh in range(2):
        qh = jnp.where((lane >= hh * C_HEAD_DIM) & (lane < (hh + 1) * C_HEAD_DIM), qf, 0.0)
        qhb = qh.astype(_BF16)
        gate = lax.dot_general(kmean_ref[...], qh, _NT, precision=lax.Precision.HIGHEST,
                               preferred_element_type=_F32)
        gate = jnp.where(blk < qi, gate, neg)
        rank = jnp.zeros(gate.shape, jnp.int32)
        for j2 in range(nblk):
            other = gate[j2:j2 + 1, :]
            ahead = (other > gate) | ((other == gate) & (j2 < blk))
            rank = rank + ahead.astype(jnp.int32)
        bias_ref[...] = jnp.where((blk < qi) & (rank < C_TOP_K), 0.0, neg)

        st = lax.dot_general(kb_ref[qi], qhb, _NT, preferred_element_type=_F32) * scale
        st = jnp.where(key_i <= qry_i, st, neg)
        m = jnp.max(st, axis=0, keepdims=True)
        pt = jnp.exp(st - m)
        l = jnp.sum(pt, axis=0, keepdims=True)
        acc = jnp.dot(vt_ref[qi], pt.astype(_BF16), preferred_element_type=_F32)

        def past(j, carry):
            m, l, acc = carry
            st = lax.dot_general(kb_ref[j], qhb, _NT, preferred_element_type=_F32) * scale
            st = st + bias_ref[pl.ds(j, 1), :]
            m_new = jnp.maximum(m, jnp.max(st, axis=0, keepdims=True))
            alpha = jnp.exp(m - m_new)
            pt = jnp.exp(st - m_new)
            l = alpha * l + jnp.sum(pt, axis=0, keepdims=True)
            acc = alpha * acc + jnp.dot(vt_ref[j], pt.astype(_BF16), preferred_element_type=_F32)
            return m_new, l, acc

        m, l, acc = lax.fori_loop(0, qi, past, (m, l, acc))
        heads.append(acc / l)
    sub = lax.broadcasted_iota(jnp.int32, heads[0].shape, 0)
    o_ref[0] = jnp.where(sub < C_HEAD_DIM, heads[0], heads[1]).T


def _prompt_attn(q, k, v):
    nb, s, _ = q.shape
    nblk = s // C_BLOCK
    npair = C_WIDTH // LANES
    qspec = pl.BlockSpec((1, C_BLOCK, LANES), lambda b, hp, i: (b, i, hp))
    kvspec = pl.BlockSpec((1, s, LANES), lambda b, hp, i: (b, 0, hp))
    return pl.pallas_call(
        functools.partial(_prompt_attn_body, nblk=nblk),
        grid=(nb, npair, nblk), in_specs=[qspec, kvspec, kvspec], out_specs=qspec,
        out_shape=jax.ShapeDtypeStruct((nb, s, C_WIDTH), _F32),
        scratch_shapes=[pltpu.VMEM((nblk, C_BLOCK, LANES), _BF16),
                        pltpu.VMEM((nblk, LANES, C_BLOCK), _BF16),
                        pltpu.VMEM((nblk, LANES), _F32),
                        pltpu.VMEM((nblk, C_BLOCK), _F32)],
        compiler_params=pltpu.CompilerParams(
            dimension_semantics=("arbitrary", "arbitrary", "arbitrary"),
            vmem_limit_bytes=VMEM_LIMIT),
        name="prompt_attn",
    )(q, k, v)


_KM_PAGES = 16


def _kmean_body(pt_ref, ck_ref, o_ref, buf_ref, acc_ref, sem, *, nseq, npages):
    nchunk = npages // _KM_PAGES
    step = pl.program_id(0) * nseq + pl.program_id(1)
    total = pl.num_programs(0) * nseq * nchunk

    def copies(g, slot):
        st = g // nchunk
        c = g % nchunk
        layer = st // nseq
        b = st % nseq
        out = []
        for p in range(_KM_PAGES):
            phys = pt_ref[b, c * _KM_PAGES + p]
            out.append(pltpu.make_async_copy(ck_ref.at[layer, phys], buf_ref.at[slot, p],
                                             sem.at[slot]))
        return out

    @pl.when(step == 0)
    def _():
        for cp in copies(0, 0):
            cp.start()

    acc_ref[...] = jnp.zeros(acc_ref.shape, _F32)
    lane = lax.broadcasted_iota(jnp.int32, acc_ref.shape, 2)

    def chunk(c, _):
        g = step * nchunk + c
        slot = g % 2

        @pl.when(g + 1 < total)
        def _():
            for cp in copies(g + 1, 1 - slot):
                cp.start()

        for cp in copies(g, slot):
            cp.wait()
        for jj in range(_KM_PAGES // PAGES_PER_BLOCK):
            blk = buf_ref[slot, PAGES_PER_BLOCK * jj]
            for pp in range(1, PAGES_PER_BLOCK):
                blk = blk + buf_ref[slot, PAGES_PER_BLOCK * jj + pp]
            mean = jnp.sum(blk, axis=-1, keepdims=True) * (1.0 / C_BLOCK)
            j = c * (_KM_PAGES // PAGES_PER_BLOCK) + jj
            acc_ref[...] = jnp.where(lane == j, mean, acc_ref[...])
        return 0

    lax.fori_loop(0, nchunk, chunk, 0)
    o_ref[0, 0] = acc_ref[...]


def _kmean(page_table, ck):
    nlayer = ck.shape[0]
    nseq, npages = page_table.shape
    slab = ck.shape[2:]
    return pl.pallas_call(
        functools.partial(_kmean_body, nseq=nseq, npages=npages),
        grid_spec=pltpu.PrefetchScalarGridSpec(
            num_scalar_prefetch=1, grid=(nlayer, nseq),
            in_specs=[pl.BlockSpec(memory_space=pl.ANY)],
            out_specs=pl.BlockSpec((1, 1) + slab, lambda l, b, pt: (l, b, 0, 0, 0)),
            scratch_shapes=[pltpu.VMEM((2, _KM_PAGES) + slab, _F32),
                            pltpu.VMEM(slab, _F32),
                            pltpu.SemaphoreType.DMA((2,))]),
        out_shape=jax.ShapeDtypeStruct((nlayer, nseq) + slab, _F32),
        compiler_params=pltpu.CompilerParams(
            dimension_semantics=("arbitrary", "arbitrary"), vmem_limit_bytes=VMEM_LIMIT),
        name="kmean",
    )(page_table, ck)


def _sample_select_body(qt_ref, km_ref, o_ref, *, nq, nblk):
    lane = lax.broadcasted_iota(jnp.int32, (1, LANES), 1)
    lane_f = lane.astype(_F32)
    sub = lax.broadcasted_iota(jnp.int32, (8, LANES), 0)
    neg = jnp.float32(-jnp.inf)
    for h in range(C_HEADS):
        km = km_ref[0, 0, h]
        out = jnp.zeros((8, LANES), _F32)
        for t in range(nq):
            gate = jnp.sum(km * qt_ref[0, h][:, t:t + 1], axis=0, keepdims=True)
            gate = jnp.where(lane < nblk, gate, neg)
            row = jnp.zeros((1, LANES), _F32)
            for r in range(C_TOP_K):
                best = jnp.max(gate, axis=1, keepdims=True)
                idx = jnp.min(jnp.where(gate == best, lane_f, float(LANES)), axis=1,
                              keepdims=True)
                gate = jnp.where(lane_f == idx, neg, gate)
                row = jnp.where(lane == r, idx, row)
            out = jnp.where(sub == t, row, out)
        o_ref[0, h] = out.astype(jnp.int32)


def _sample_select(qt, kmt, layer, *, nq, nblk):
    nseq = qt.shape[0]
    slab = kmt.shape[2:]
    return pl.pallas_call(
        functools.partial(_sample_select_body, nq=nq, nblk=nblk),
        grid=(nseq,),
        in_specs=[pl.BlockSpec((1,) + qt.shape[1:], lambda b: (b, 0, 0, 0)),
                  pl.BlockSpec((1, 1) + slab, lambda b: (layer, b, 0, 0, 0))],
        out_specs=pl.BlockSpec((1, C_HEADS, 8, LANES), lambda b: (b, 0, 0, 0)),
        out_shape=jax.ShapeDtypeStruct((nseq, C_HEADS, 8, LANES), jnp.int32),
        compiler_params=pltpu.CompilerParams(dimension_semantics=("arbitrary",)),
        name="sample_select",
    )(qt, kmt)


def _sample_attn_body(idx_ref, pt_ref, ck_ref, cv_ref, qt_ref, kn_ref, vn_ref, o_ref,
                      kbuf_ref, vbuf_ref, sem, *, layer, nq):
    b = pl.program_id(0)
    nseq = pl.num_programs(0)
    nslab = C_TOP_K * PAGES_PER_BLOCK
    scale = C_HEAD_DIM ** -0.5

    def for_each_copy(bb, slot, fn):
        def per_head(h, _):
            for t in range(nq):
                for r in range(C_TOP_K):
                    blk = idx_ref[((bb * C_HEADS + h) * nq + t) * C_TOP_K + r]
                    for pg in range(PAGES_PER_BLOCK):
                        phys = pt_ref[bb, blk * PAGES_PER_BLOCK + pg]
                        dst = (slot, t * C_HEADS + h, r * PAGES_PER_BLOCK + pg)
                        fn(pltpu.make_async_copy(ck_ref.at[layer, phys, h], kbuf_ref.at[dst],
                                                 sem.at[slot]))
                        fn(pltpu.make_async_copy(cv_ref.at[layer, phys, h], vbuf_ref.at[dst],
                                                 sem.at[slot]))
            return 0
        lax.fori_loop(0, C_HEADS, per_head, 0)

    slot = b % 2

    @pl.when(b == 0)
    def _():
        for_each_copy(b, slot, lambda cp: cp.start())

    @pl.when(b + 1 < nseq)
    def _():
        for_each_copy(b + 1, 1 - slot, lambda cp: cp.start())

    for_each_copy(b, slot, lambda cp: cp.wait())

    lane = lax.broadcasted_iota(jnp.int32, (1, LANES), 1)
    lane_o = lax.broadcasted_iota(jnp.int32, (C_HEAD_DIM, LANES), 1)
    neg = jnp.float32(-jnp.inf)

    def per_head(h, _):
        qt = qt_ref[0, h]
        kn = kn_ref[0, h]
        vn = vn_ref[0, h]
        out = jnp.zeros((C_HEAD_DIM, LANES), _F32)
        for t in range(nq):
            qcol = qt[:, t:t + 1]
            s_own = jnp.sum(kn * qcol, axis=0, keepdims=True) * scale
            s_own = jnp.where(lane <= t, s_own, neg)
            m = jnp.max(s_own, axis=1, keepdims=True)
            s_sel = []
            for sl in range(nslab):
                s = jnp.sum(kbuf_ref[slot, t * C_HEADS + h, sl] * qcol, axis=0,
                            keepdims=True) * scale
                s_sel.append(s)
                m = jnp.maximum(m, jnp.max(s, axis=1, keepdims=True))
            p_own = jnp.exp(s_own - m)
            den = jnp.sum(p_own, axis=1, keepdims=True)
            acc = vn * p_own
            for sl in range(nslab):
                p = jnp.exp(s_sel[sl] - m)
                den = den + jnp.sum(p, axis=1, keepdims=True)
                acc = acc + vbuf_ref[slot, t * C_HEADS + h, sl] * p
            col = jnp.sum(acc, axis=1, keepdims=True) / den
            out = jnp.where(lane_o == t, col, out)
        o_ref[0, h] = out
        return 0

    lax.fori_loop(0, C_HEADS, per_head, 0)


def _sample_attn(idx, page_table, ck, cv, qt, knt, vnt, layer, *, nq):
    nseq = qt.shape[0]
    slab = qt.shape[1:]
    nslab = C_TOP_K * PAGES_PER_BLOCK
    bspec = pl.BlockSpec((1,) + slab, lambda b, idx, pt: (b, 0, 0, 0))
    any_spec = pl.BlockSpec(memory_space=pl.ANY)
    buf = pltpu.VMEM((2, nq * C_HEADS, nslab, C_HEAD_DIM, PAGE_SIZE), _F32)
    return pl.pallas_call(
        functools.partial(_sample_attn_body, layer=layer, nq=nq),
        grid_spec=pltpu.PrefetchScalarGridSpec(
            num_scalar_prefetch=2, grid=(nseq,),
            in_specs=[any_spec, any_spec, bspec, bspec, bspec],
            out_specs=bspec,
            scratch_shapes=[buf, buf, pltpu.SemaphoreType.DMA((2,))]),
        out_shape=jax.ShapeDtypeStruct((nseq,) + slab, _F32),
        compiler_params=pltpu.CompilerParams(
            dimension_semantics=("arbitrary",), vmem_limit_bytes=VMEM_LIMIT),
        name="sample_attn",
    )(idx, page_table, ck, cv, qt, knt, vnt)


def _rope_tables(pos):
    half = ROPE_DIM // 2
    inv = 1.0 / (ROPE_THETA ** (jnp.arange(half, dtype=_F32) / half))
    ang = pos.astype(_F32)[:, None] * inv[None, :]
    cos, sin = jnp.cos(ang), jnp.sin(ang)
    n = pos.shape[0]
    rest = C_HEAD_DIM - ROPE_DIM
    one, zero, z8 = jnp.ones((n, rest), _F32), jnp.zeros((n, rest), _F32), jnp.zeros((n, half), _F32)
    rep = LANES // C_HEAD_DIM
    cos_t = jnp.tile(jnp.concatenate([cos, cos, one], axis=1), (1, rep))
    sa_t = jnp.tile(jnp.concatenate([-sin, z8, zero], axis=1), (1, rep))
    sb_t = jnp.tile(jnp.concatenate([z8, sin, zero], axis=1), (1, rep))
    return cos_t, sa_t, sb_t


def _to_lane_minor(z, nq, nseq):
    zt = z.reshape(nq, nseq, C_HEADS, C_HEAD_DIM).transpose(1, 2, 3, 0)
    return jnp.pad(zt, ((0, 0), (0, 0), (0, 0), (0, LANES - nq)))


def kernel(x_prompt, x_sample, state_pool, cache_k, cache_v, page_table, g_mix, w_in, g_av,
           w_spatial, b_spatial, w_pool, pool_scale, w_br_a, w_br_b, w_br_c, w_out, g_ffn,
           w_ff1, w_ff2, g_final):
    depth = w_in.shape[0]
    nbp, seq, _ = x_prompt.shape
    nseq, nq, _ = x_sample.shape
    past_len = page_table.shape[1] * PAGE_SIZE
    nblk_past = past_len // C_BLOCK
    rows_s = nseq * nq
    tm_p = 256

    p = {
        'g_mix': g_mix[:, None, :], 'w_in': w_in.astype(_BF16), 'g_av': g_av[:, None, :],
        'w_pool': w_pool.astype(_BF16), 'pool_scale': pool_scale[:, None, :],
        'w_br_a': w_br_a.astype(_BF16), 'w_br_b': w_br_b.astype(_BF16),
        'w_br_c': w_br_c.astype(_BF16), 'w_out': w_out.astype(_BF16),
        'g_ffn': g_ffn[:, None, :], 'w_ff1': w_ff1.astype(_BF16), 'w_ff2': w_ff2.astype(_BF16),
    }
    g_fin = g_final[None, :]

    causal = jnp.tril(jnp.ones((A_CHUNK, A_CHUNK), dtype=bool))
    wsp_p = jnp.where(causal[None, None], w_spatial, 0).astype(_BF16)
    eye = jnp.eye(nseq, dtype=_F32)
    wsp_s = jnp.einsum('lgts,bc->lgtbsc', jnp.where(causal[None, None], w_spatial, 0)[:, :, :nq, :nq],
                       eye).reshape(depth, A_GROUPS, rows_s, rows_s).astype(_BF16)
    bias_rows = jnp.repeat(b_spatial.transpose(0, 2, 1), A_WIDTH // A_GROUPS, axis=2)
    bsp_p = jnp.tile(bias_rows, (1, tm_p // A_CHUNK, 1))
    bsp_s = jnp.repeat(bias_rows[:, :nq], nseq, axis=1)

    rope_p = _rope_tables(jnp.arange(seq))
    rope_s = _rope_tables(past_len + jnp.arange(rows_s) // nseq)

    ck = cache_k.transpose(0, 1, 3, 4, 2)
    cv = cache_v.transpose(0, 1, 3, 4, 2)
    kmt = _kmean(page_table, ck)

    zero_prefix = jnp.zeros((1, POOL_STEPS, B_WIDTH), _F32)
    xp = x_prompt
    xs = x_sample.transpose(1, 0, 2).reshape(1, rows_s, D_MODEL)
    kp_l, vp_l, ks_l, vs_l, pp_l, ps_l, av_l = [], [], [], [], [], [], []
    for l in range(depth):
        last = l == depth - 1
        mab, gc, q, k, v, _, tail = _mix_in(xp, l, p, wsp_p, bsp_p[l], zero_prefix, *rope_p,
                                            tm=tm_p, stride=1, pos0=0)
        c = _prompt_attn(q, k, v)
        xp = _mix_out(mab, gc, c, xp, l, p, g_fin, tm=tm_p, final=last)
        kp_l.append(k.reshape(nbp, seq // PAGE_SIZE, PAGE_SIZE, C_HEADS, C_HEAD_DIM))
        vp_l.append(v.reshape(nbp, seq // PAGE_SIZE, PAGE_SIZE, C_HEADS, C_HEAD_DIM))
        pp_l.append(tail[:, -1, 1:, :])

        prefix = jnp.pad(state_pool[l].transpose(1, 0, 2), ((1, 0), (0, 0), (0, 0)))
        prefix = prefix.reshape(1, POOL_STEPS * nseq, B_WIDTH)
        mab, gc, q, k, v, vn, tail = _mix_in(xs, l, p, wsp_s, bsp_s[l], prefix, *rope_s,
                                             tm=rows_s, stride=nseq, pos0=past_len)
        qt = _to_lane_minor(q[0], nq, nseq)
        sel = _sample_select(qt, kmt, l, nq=nq, nblk=nblk_past)
        idx = sel[:, :, :nq, :C_TOP_K].reshape(-1)
        ct = _sample_attn(idx, page_table, ck, cv, qt, _to_lane_minor(k[0], nq, nseq),
                          _to_lane_minor(v[0], nq, nseq), l, nq=nq)
        c = ct[..., :nq].transpose(3, 0, 1, 2).reshape(1, rows_s, C_WIDTH)
        xs = _mix_out(mab, gc, c, xs, l, p, g_fin, tm=rows_s, final=last)
        ks_l.append(k[0].reshape(nq, nseq, C_HEADS, C_HEAD_DIM).transpose(1, 0, 2, 3))
        vs_l.append(v[0].reshape(nq, nseq, C_HEADS, C_HEAD_DIM).transpose(1, 0, 2, 3))
        ps_l.append(tail[0, 0].reshape(POOL_STEPS, nseq, B_WIDTH)[1:].transpose(1, 0, 2))
        av_l.append(vn[0].reshape(nq, nseq, A_WIDTH).transpose(1, 0, 2))

    y_prompt = xp
    y_sample = xs[0].reshape(nq, nseq, D_MODEL).transpose(1, 0, 2)
    return (y_prompt, y_sample, jnp.stack(kp_l), jnp.stack(vp_l), jnp.stack(ks_l), jnp.stack(vs_l),
            jnp.stack(pp_l), jnp.stack(ps_l), jnp.stack(av_l))
```

```python
import functools

import jax
import jax.numpy as jnp
from jax import lax
from jax.experimental import pallas as pl
from jax.experimental.pallas import tpu as pltpu

D_MODEL = 1024
A_WIDTH = 512
A_GROUPS = 4
A_CHUNK = 128
B_WIDTH = 512
B_WINDOWS = (2, 4, 8, 16)
B_GROUP_DIM = 128
B_STATE = 15
C_HEADS = 8
C_HEAD_DIM = 64
C_WIDTH = 512
C_BLOCK = 256
C_TOP_K = 3
ROPE_DIM = 16
ROPE_THETA = 500000.0
D_FF = 4096
EPS = 1e-6
PAGE_SIZE = 128
PAGES_PER_BLOCK = C_BLOCK // PAGE_SIZE
POOL_STEPS = B_STATE + 1
LANES = 128
HEADS_PER_TILE = LANES // C_HEAD_DIM
VMEM_LIMIT = 56 * 1024 * 1024

_F32 = jnp.float32
_BF16 = jnp.bfloat16
_NT = (((1,), (1,)), ((), ()))


def _rms(x, g):
    return x * lax.rsqrt(jnp.mean(x * x, axis=-1, keepdims=True) + EPS) * g


def _mix_in_body(*refs, tm, stride, pos0, carry, slabs, n_alias):
    (x_ref, gmix_ref, win_ref, gav_ref, wsp_ref, bsp_ref, prefix_ref, wpool_ref, pscale_ref,
     wbra_ref, wbrb_ref, cos_ref, sa_ref, sb_ref) = refs[:14]
    outs = refs[14 + n_alias:]
    if slabs:
        mab_ref, gc_ref, q_ref, k_ref, kt_ref, vt_ref, tail_ref, ext_ref = outs
    else:
        mab_ref, gc_ref, q_ref, k_ref, v_ref, vn_ref, tail_ref, ext_ref = outs
    hdr = POOL_STEPS * stride
    i = pl.program_id(1)
    x = x_ref[0]
    hb = _rms(x, gmix_ref[0]).astype(_BF16)

    def seg(lo, hi):
        return jnp.dot(hb, win_ref[0, :, lo:hi], preferred_element_type=_F32)

    u = jax.nn.gelu(seg(0, A_WIDTH), approximate=True)
    vn = _rms(jax.nn.gelu(seg(A_WIDTH, 2 * A_WIDTH), approximate=True), gav_ref[0])
    if not slabs:
        vn_ref[0] = vn
    vnb = vn.astype(_BF16)
    chunks = []
    for c in range(tm // A_CHUNK):
        rows = slice(c * A_CHUNK, (c + 1) * A_CHUNK)
        groups = [jnp.dot(wsp_ref[0, g], vnb[rows, g * LANES:(g + 1) * LANES],
                          preferred_element_type=_F32) for g in range(A_GROUPS)]
        chunks.append(jnp.concatenate(groups, axis=1))
    mixed = chunks[0] if len(chunks) == 1 else jnp.concatenate(chunks, axis=0)
    a_out = u * (mixed + bsp_ref[...])

    xb = seg(2 * A_WIDTH, 2 * A_WIDTH + B_WIDTH)
    if carry:
        @pl.when(i == 0)
        def _():
            ext_ref[0:hdr, :] = prefix_ref[0]

        @pl.when(i > 0)
        def _():
            ext_ref[0:hdr, :] = ext_ref[tm:tm + hdr, :]
    else:
        ext_ref[0:hdr, :] = prefix_ref[0]
    ext_ref[hdr:hdr + tm, :] = xb
    tail_ref[0, 0] = ext_ref[tm:tm + hdr, :]
    row = lax.broadcasted_iota(jnp.int32, (tm, 1), 0)
    pos = pos0 + lax.shift_right_logical(i * tm + row, stride.bit_length() - 1)
    pooled = []
    for gi, w in enumerate(B_WINDOWS):
        cols = slice(gi * B_GROUP_DIM, (gi + 1) * B_GROUP_DIM)
        cur = xb[:, cols]
        s = cur
        for kk in range(1, w):
            s = s + ext_ref[hdr - kk * stride:hdr - kk * stride + tm, cols]
        cnt = jnp.minimum(w, pos + 1).astype(_F32)
        pg = (s / cnt - cur).astype(_BF16)
        pooled.append(jnp.dot(pg, wpool_ref[0, gi], preferred_element_type=_F32))
    b_out = jnp.concatenate(pooled, axis=1) * pscale_ref[0]

    g0 = 2 * A_WIDTH + B_WIDTH + 3 * C_WIDTH
    ga = jax.nn.sigmoid(seg(g0, g0 + D_MODEL))
    gb = jax.nn.sigmoid(seg(g0 + D_MODEL, g0 + 2 * D_MODEL))
    gc_ref[0] = jax.nn.sigmoid(seg(g0 + 2 * D_MODEL, g0 + 3 * D_MODEL))
    mab_ref[0] = (ga * jnp.dot(a_out.astype(_BF16), wbra_ref[0], preferred_element_type=_F32)
                  + gb * jnp.dot(b_out.astype(_BF16), wbrb_ref[0], preferred_element_type=_F32))

    c0 = 2 * A_WIDTH + B_WIDTH
    cosv, sav, sbv = cos_ref[...], sa_ref[...], sb_ref[...]

    def rope(z):
        parts = []
        for cb in range(C_WIDTH // LANES):
            zc = z[:, cb * LANES:(cb + 1) * LANES]
            parts.append(zc * cosv
                         + pltpu.roll(zc, LANES - ROPE_DIM // 2, axis=1) * sav
                         + pltpu.roll(zc, ROPE_DIM // 2, axis=1) * sbv)
        return jnp.concatenate(parts, axis=1)

    q_ref[0] = rope(seg(c0, c0 + C_WIDTH))
    k = rope(seg(c0 + C_WIDTH, c0 + 2 * C_WIDTH))
    v = seg(c0 + 2 * C_WIDTH, c0 + 3 * C_WIDTH)
    k_ref[0] = k
    if slabs:
        for pg in range(tm // PAGE_SIZE):
            rows = slice(pg * PAGE_SIZE, (pg + 1) * PAGE_SIZE)
            kt_ref[0, 0, pg] = k[rows, :].T.reshape(C_HEADS, C_HEAD_DIM, PAGE_SIZE)
            vt_ref[0, 0, pg] = v[rows, :].T.reshape(C_HEADS, C_HEAD_DIM, PAGE_SIZE)
    else:
        v_ref[0] = v


def _const_spec(shape, layer):
    nd = len(shape)
    return pl.BlockSpec((1,) + tuple(shape[1:]), lambda *_: (layer,) + (0,) * (nd - 1),
                        pipeline_mode=pl.Buffered(1))


def _mix_in(x, layer, p, wsp, bsp, prefix, cos_t, sa_t, sb_t, *, tm, stride, pos0,
            slab_shape=None, slab_prev=()):
    nb, s, _ = x.shape
    nt = s // tm
    hdr = POOL_STEPS * stride
    row3 = lambda width: pl.BlockSpec((1, tm, width), lambda b, i: (b, i, 0))
    tab = pl.BlockSpec((tm, LANES), lambda b, i: (i, 0))
    in_specs = [
        row3(D_MODEL),
        _const_spec(p['g_mix'].shape, layer),
        _const_spec(p['w_in'].shape, layer),
        _const_spec(p['g_av'].shape, layer),
        _const_spec(wsp.shape, layer),
        pl.BlockSpec((tm, A_WIDTH), lambda b, i: (0, 0), pipeline_mode=pl.Buffered(1)),
        pl.BlockSpec((1, hdr, B_WIDTH), lambda b, i: (0, 0, 0), pipeline_mode=pl.Buffered(1)),
        _const_spec(p['w_pool'].shape, layer),
        _const_spec(p['pool_scale'].shape, layer),
        _const_spec(p['w_br_a'].shape, layer),
        _const_spec(p['w_br_b'].shape, layer),
        tab, tab, tab,
    ] + [pl.BlockSpec(memory_space=pl.ANY)] * len(slab_prev)
    f = lambda width: jax.ShapeDtypeStruct((nb, s, width), _F32)
    tail_spec = pl.BlockSpec((1, 1, hdr, B_WIDTH), lambda b, i: (b, i, 0, 0))
    tail_shape = jax.ShapeDtypeStruct((nb, nt, hdr, B_WIDTH), _F32)
    aliases = {}
    if slab_shape is not None:
        slab_spec = pl.BlockSpec((1, 1, tm // PAGE_SIZE) + tuple(slab_shape[3:]),
                                 lambda b, i: (layer, b, i, 0, 0, 0))
        slab = jax.ShapeDtypeStruct(slab_shape, _F32)
        out_specs = [row3(D_MODEL), row3(D_MODEL), row3(C_WIDTH), row3(C_WIDTH),
                     slab_spec, slab_spec, tail_spec]
        out_shape = [f(D_MODEL), f(D_MODEL), f(C_WIDTH), f(C_WIDTH), slab, slab, tail_shape]
        if slab_prev:
            aliases = {14: 4, 15: 5}
    else:
        out_specs = [row3(D_MODEL), row3(D_MODEL), row3(C_WIDTH), row3(C_WIDTH), row3(C_WIDTH),
                     row3(A_WIDTH), tail_spec]
        out_shape = [f(D_MODEL), f(D_MODEL), f(C_WIDTH), f(C_WIDTH), f(C_WIDTH), f(A_WIDTH),
                     tail_shape]
    body = functools.partial(_mix_in_body, tm=tm, stride=stride, pos0=pos0, carry=nt > 1,
                             slabs=slab_shape is not None, n_alias=len(slab_prev))
    return pl.pallas_call(
        body, grid=(nb, nt), in_specs=in_specs, out_specs=out_specs, out_shape=out_shape,
        scratch_shapes=[pltpu.VMEM((hdr + tm, B_WIDTH), _F32)],
        input_output_aliases=aliases,
        compiler_params=pltpu.CompilerParams(
            dimension_semantics=("arbitrary", "arbitrary"), vmem_limit_bytes=VMEM_LIMIT),
        name="mix_in",
    )(x, p['g_mix'], p['w_in'], p['g_av'], wsp, bsp, prefix, p['w_pool'], p['pool_scale'],
      p['w_br_a'], p['w_br_b'], cos_t, sa_t, sb_t, *slab_prev)


def _mix_out_body(mab_ref, gc_ref, c_ref, x_ref, wbrc_ref, wout_ref, gffn_ref, wff1_ref, wff2_ref,
                  gfin_ref, o_ref, *, final, ff_chunk):
    merged = mab_ref[0] + gc_ref[0] * jnp.dot(c_ref[0].astype(_BF16), wbrc_ref[0],
                                              preferred_element_type=_F32)
    x1 = x_ref[0] + jnp.dot(merged.astype(_BF16), wout_ref[0], preferred_element_type=_F32)
    h2 = _rms(x1, gffn_ref[0]).astype(_BF16)
    ff = None
    for c in range(D_FF // ff_chunk):
        cols = slice(c * ff_chunk, (c + 1) * ff_chunk)
        f = jnp.dot(h2, wff1_ref[0, :, cols], preferred_element_type=_F32)
        f = jnp.square(jnp.maximum(f, 0.0)).astype(_BF16)
        part = jnp.dot(f, wff2_ref[0, cols, :], preferred_element_type=_F32)
        ff = part if ff is None else ff + part
    x2 = x1 + ff
    o_ref[0] = _rms(x2, gfin_ref[...]) if final else x2


def _mix_out(mab, gc, c, x, layer, p, g_final, *, tm, final):
    nb, s, _ = x.shape
    row3 = lambda width: pl.BlockSpec((1, tm, width), lambda b, i: (b, i, 0))
    in_specs = [
        row3(D_MODEL), row3(D_MODEL), row3(C_WIDTH), row3(D_MODEL),
        _const_spec(p['w_br_c'].shape, layer),
        _const_spec(p['w_out'].shape, layer),
        _const_spec(p['g_ffn'].shape, layer),
        _const_spec(p['w_ff1'].shape, layer),
        _const_spec(p['w_ff2'].shape, layer),
        pl.BlockSpec((1, D_MODEL), lambda b, i: (0, 0), pipeline_mode=pl.Buffered(1)),
    ]
    body = functools.partial(_mix_out_body, final=final, ff_chunk=1024)
    return pl.pallas_call(
        body, grid=(nb, s // tm), in_specs=in_specs, out_specs=row3(D_MODEL),
        out_shape=jax.ShapeDtypeStruct((nb, s, D_MODEL), _F32),
        compiler_params=pltpu.CompilerParams(
            dimension_semantics=("arbitrary", "arbitrary"), vmem_limit_bytes=VMEM_LIMIT),
        name="mix_out",
    )(mab, gc, c, x, p['w_br_c'], p['w_out'], p['g_ffn'], p['w_ff1'], p['w_ff2'], g_final)


def _prompt_attn_body(q_ref, k_ref, vt_ref, o_ref, kb_ref, vtb_ref, kmean_ref, *, nblk):
    scale = C_HEAD_DIM ** -0.5
    neg = jnp.float32(-jnp.inf)
    for j in range(nblk):
        rows = slice(j * C_BLOCK, (j + 1) * C_BLOCK)
        kf = k_ref[0, rows, :]
        kb_ref[rows, :] = kf.astype(_BF16)
        kmean_ref[j:j + 1, :] = jnp.sum(kf, axis=0, keepdims=True) * (1.0 / C_BLOCK)
    for pg in range(nblk * PAGES_PER_BLOCK):
        vtb_ref[:, pg * PAGE_SIZE:(pg + 1) * PAGE_SIZE] = (
            vt_ref[0, 0, pg].reshape(LANES, PAGE_SIZE).astype(_BF16))

    lane = lax.broadcasted_iota(jnp.int32, (C_BLOCK, LANES), 1)
    blk = lax.broadcasted_iota(jnp.int32, (nblk, C_BLOCK), 0)
    causal = (lax.broadcasted_iota(jnp.int32, (C_BLOCK, C_BLOCK), 0)
              <= lax.broadcasted_iota(jnp.int32, (C_BLOCK, C_BLOCK), 1))
    sub = lax.broadcasted_iota(jnp.int32, (LANES, C_BLOCK), 0)
    for qi in range(nblk):
        qf = q_ref[0, qi * C_BLOCK:(qi + 1) * C_BLOCK, :]
        nk = (qi + 1) * C_BLOCK
        heads = []
        for hh in range(HEADS_PER_TILE):
            qh = jnp.where((lane >= hh * C_HEAD_DIM) & (lane < (hh + 1) * C_HEAD_DIM), qf, 0.0)
            qhb = (qh * scale).astype(_BF16)
            st = lax.dot_general(kb_ref[0:nk, :], qhb, _NT, preferred_element_type=_F32)
            bias = None
            if qi > C_TOP_K:
                gate = lax.dot_general(kmean_ref[...], qh, _NT, precision=lax.Precision.HIGHEST,
                                       preferred_element_type=_F32)
                gate = jnp.where(blk < qi, gate, neg)
                rank = jnp.zeros(gate.shape, jnp.int32)
                for j2 in range(qi):
                    other = gate[j2:j2 + 1, :]
                    ahead = (other > gate) | ((other == gate) & (j2 < blk))
                    rank = rank + ahead.astype(jnp.int32)
                bias = jnp.where(rank < C_TOP_K, 0.0, neg)
            pieces = []
            for j in range(qi):
                sj = st[j * C_BLOCK:(j + 1) * C_BLOCK, :]
                pieces.append(sj if bias is None else sj + bias[j:j + 1, :])
            pieces.append(jnp.where(causal, st[qi * C_BLOCK:nk, :], neg))
            m = jnp.max(pieces[0], axis=0, keepdims=True)
            for sj in pieces[1:]:
                m = jnp.maximum(m, jnp.max(sj, axis=0, keepdims=True))
            probs = [jnp.exp(sj - m) for sj in pieces]
            den = jnp.sum(probs[0], axis=0, keepdims=True)
            for pj in probs[1:]:
                den = den + jnp.sum(pj, axis=0, keepdims=True)
            pb = probs[0] if len(probs) == 1 else jnp.concatenate(probs, axis=0)
            acc = jnp.dot(vtb_ref[:, 0:nk], pb.astype(_BF16), preferred_element_type=_F32)
            heads.append(acc / den)
        out_t = heads[0]
        for hh in range(1, HEADS_PER_TILE):
            out_t = jnp.where(sub < hh * C_HEAD_DIM, out_t, heads[hh])
        o_ref[0, qi * C_BLOCK:(qi + 1) * C_BLOCK, :] = out_t.T


def _prompt_attn(q, k, vt_all, layer):
    nb, s, _ = q.shape
    nblk = s // C_BLOCK
    npair = C_WIDTH // LANES
    qspec = pl.BlockSpec((1, s, LANES), lambda b, hp: (b, 0, hp))
    vtspec = pl.BlockSpec((1, 1, s // PAGE_SIZE, HEADS_PER_TILE, C_HEAD_DIM, PAGE_SIZE),
                          lambda b, hp: (layer, b, 0, hp, 0, 0))
    return pl.pallas_call(
        functools.partial(_prompt_attn_body, nblk=nblk),
        grid=(nb, npair), in_specs=[qspec, qspec, vtspec], out_specs=qspec,
        out_shape=jax.ShapeDtypeStruct((nb, s, C_WIDTH), _F32),
        scratch_shapes=[pltpu.VMEM((s, LANES), _BF16),
                        pltpu.VMEM((LANES, s), _BF16),
                        pltpu.VMEM((nblk, LANES), _F32)],
        compiler_params=pltpu.CompilerParams(
            dimension_semantics=("arbitrary", "arbitrary"), vmem_limit_bytes=VMEM_LIMIT),
        name="prompt_attn",
    )(q, k, vt_all)


_KM_PAGES = 16


def _kmean_body(pt_ref, ck_ref, o_ref, buf_ref, acc_ref, sem, *, nseq, npages):
    nchunk = npages // _KM_PAGES
    step = pl.program_id(0) * nseq + pl.program_id(1)
    total = pl.num_programs(0) * nseq * nchunk

    def copies(g, slot):
        st = g // nchunk
        c = g % nchunk
        layer = st // nseq
        b = st % nseq
        out = []
        for p in range(_KM_PAGES):
            phys = pt_ref[b, c * _KM_PAGES + p]
            out.append(pltpu.make_async_copy(ck_ref.at[layer, phys], buf_ref.at[slot, p],
                                             sem.at[slot]))
        return out

    @pl.when(step == 0)
    def _():
        for cp in copies(0, 0):
            cp.start()

    acc_ref[...] = jnp.zeros(acc_ref.shape, _F32)
    lane = lax.broadcasted_iota(jnp.int32, acc_ref.shape, 2)

    def chunk(c, _):
        g = step * nchunk + c
        slot = g % 2

        @pl.when(g + 1 < total)
        def _():
            for cp in copies(g + 1, 1 - slot):
                cp.start()

        for cp in copies(g, slot):
            cp.wait()
        for jj in range(_KM_PAGES // PAGES_PER_BLOCK):
            blk = buf_ref[slot, PAGES_PER_BLOCK * jj]
            for pp in range(1, PAGES_PER_BLOCK):
                blk = blk + buf_ref[slot, PAGES_PER_BLOCK * jj + pp]
            mean = jnp.sum(blk, axis=-1, keepdims=True) * (1.0 / C_BLOCK)
            j = c * (_KM_PAGES // PAGES_PER_BLOCK) + jj
            acc_ref[...] = jnp.where(lane == j, mean, acc_ref[...])
        return 0

    lax.fori_loop(0, nchunk, chunk, 0)
    o_ref[0, 0] = acc_ref[...]


def _kmean(page_table, ck):
    nlayer = ck.shape[0]
    nseq, npages = page_table.shape
    slab = ck.shape[2:]
    return pl.pallas_call(
        functools.partial(_kmean_body, nseq=nseq, npages=npages),
        grid_spec=pltpu.PrefetchScalarGridSpec(
            num_scalar_prefetch=1, grid=(nlayer, nseq),
            in_specs=[pl.BlockSpec(memory_space=pl.ANY)],
            out_specs=pl.BlockSpec((1, 1) + slab, lambda l, b, pt: (l, b, 0, 0, 0)),
            scratch_shapes=[pltpu.VMEM((2, _KM_PAGES) + slab, _F32),
                            pltpu.VMEM(slab, _F32),
                            pltpu.SemaphoreType.DMA((2,))]),
        out_shape=jax.ShapeDtypeStruct((nlayer, nseq) + slab, _F32),
        compiler_params=pltpu.CompilerParams(
            dimension_semantics=("arbitrary", "arbitrary"), vmem_limit_bytes=VMEM_LIMIT),
        name="kmean",
    )(page_table, ck)


_SEL_SEQS = 4


def _sample_select_body(qt_ref, km_ref, o_ref, *, nq, nblk):
    lane = lax.broadcasted_iota(jnp.int32, (1, LANES), 1)
    lane_f = lane.astype(_F32)
    neg = jnp.float32(-jnp.inf)
    rows = []
    for sq in range(_SEL_SEQS):
        for h in range(C_HEADS):
            km = km_ref[0, sq, h]
            for t in range(nq):
                rows.append(jnp.sum(km * qt_ref[sq, h][:, t:t + 1], axis=0, keepdims=True))
    gate = jnp.where(lane < nblk, jnp.concatenate(rows, axis=0), neg)
    out = jnp.zeros(gate.shape, _F32)
    for r in range(C_TOP_K):
        best = jnp.max(gate, axis=1, keepdims=True)
        idx = jnp.min(jnp.where(gate == best, lane_f, float(LANES)), axis=1,
                      keepdims=True)
        gate = jnp.where(lane_f == idx, neg, gate)
        out = jnp.where(lane == r, idx, out)
    o_ref[...] = out.astype(jnp.int32).reshape(o_ref.shape)


def _sample_select(qt, kmt, layer, *, nq, nblk):
    nseq = qt.shape[0]
    slab = kmt.shape[2:]
    return pl.pallas_call(
        functools.partial(_sample_select_body, nq=nq, nblk=nblk),
        grid=(nseq // _SEL_SEQS,),
        in_specs=[pl.BlockSpec((_SEL_SEQS,) + qt.shape[1:], lambda b: (b, 0, 0, 0)),
                  pl.BlockSpec((1, _SEL_SEQS) + slab, lambda b: (layer, b, 0, 0, 0))],
        out_specs=pl.BlockSpec((_SEL_SEQS, C_HEADS * nq, LANES), lambda b: (b, 0, 0)),
        out_shape=jax.ShapeDtypeStruct((nseq, C_HEADS * nq, LANES), jnp.int32),
        compiler_params=pltpu.CompilerParams(dimension_semantics=("arbitrary",)),
        name="sample_select",
    )(qt, kmt)


def _sample_attn_body(idx_ref, pt_ref, ck_ref, cv_ref, qt_ref, kn_ref, vn_ref, o_ref,
                      kbuf_ref, vbuf_ref, sem, *, layer, nq):
    b = pl.program_id(0)
    nseq = pl.num_programs(0)
    nslab = C_TOP_K * PAGES_PER_BLOCK
    scale = C_HEAD_DIM ** -0.5

    def for_each_copy(bb, slot, fn):
        def per_head(h, _):
            for t in range(nq):
                for r in range(C_TOP_K):
                    blk = idx_ref[((bb * C_HEADS + h) * nq + t) * C_TOP_K + r]
                    for pg in range(PAGES_PER_BLOCK):
                        phys = pt_ref[bb, blk * PAGES_PER_BLOCK + pg]
                        dst = (slot, t * C_HEADS + h, r * PAGES_PER_BLOCK + pg)
                        fn(pltpu.make_async_copy(ck_ref.at[layer, phys, h], kbuf_ref.at[dst],
                                                 sem.at[slot]))
                        fn(pltpu.make_async_copy(cv_ref.at[layer, phys, h], vbuf_ref.at[dst],
                                                 sem.at[slot]))
            return 0
        lax.fori_loop(0, C_HEADS, per_head, 0)

    slot = b % 2

    @pl.when(b == 0)
    def _():
        for_each_copy(b, slot, lambda cp: cp.start())

    @pl.when(b + 1 < nseq)
    def _():
        for_each_copy(b + 1, 1 - slot, lambda cp: cp.start())

    for_each_copy(b, slot, lambda cp: cp.wait())

    lane = lax.broadcasted_iota(jnp.int32, (1, LANES), 1)
    lane_o = lax.broadcasted_iota(jnp.int32, (C_HEAD_DIM, LANES), 1)
    neg = jnp.float32(-jnp.inf)

    def per_head(h, _):
        qt = qt_ref[0, h]
        kn = kn_ref[0, h]
        vn = vn_ref[0, h]
        out = jnp.zeros((C_HEAD_DIM, LANES), _F32)
        for t in range(nq):
            qcol = qt[:, t:t + 1]
            s_own = jnp.sum(kn * qcol, axis=0, keepdims=True) * scale
            s_own = jnp.where(lane <= t, s_own, neg)
            m = jnp.max(s_own, axis=1, keepdims=True)
            s_sel = []
            for sl in range(nslab):
                s = jnp.sum(kbuf_ref[slot, t * C_HEADS + h, sl] * qcol, axis=0,
                            keepdims=True) * scale
                s_sel.append(s)
                m = jnp.maximum(m, jnp.max(s, axis=1, keepdims=True))
            p_own = jnp.exp(s_own - m)
            den = jnp.sum(p_own, axis=1, keepdims=True)
            acc = vn * p_own
            for sl in range(nslab):
                p = jnp.exp(s_sel[sl] - m)
                den = den + jnp.sum(p, axis=1, keepdims=True)
                acc = acc + vbuf_ref[slot, t * C_HEADS + h, sl] * p
            col = jnp.sum(acc, axis=1, keepdims=True) / den
            out = jnp.where(lane_o == t, col, out)
        o_ref[0, h] = out
        return 0

    lax.fori_loop(0, C_HEADS, per_head, 0)


def _sample_attn(idx, page_table, ck, cv, qt, knt, vnt, layer, *, nq):
    nseq = qt.shape[0]
    slab = qt.shape[1:]
    nslab = C_TOP_K * PAGES_PER_BLOCK
    bspec = pl.BlockSpec((1,) + slab, lambda b, idx, pt: (b, 0, 0, 0))
    any_spec = pl.BlockSpec(memory_space=pl.ANY)
    buf = pltpu.VMEM((2, nq * C_HEADS, nslab, C_HEAD_DIM, PAGE_SIZE), _F32)
    return pl.pallas_call(
        functools.partial(_sample_attn_body, layer=layer, nq=nq),
        grid_spec=pltpu.PrefetchScalarGridSpec(
            num_scalar_prefetch=2, grid=(nseq,),
            in_specs=[any_spec, any_spec, bspec, bspec, bspec],
            out_specs=bspec,
            scratch_shapes=[buf, buf, pltpu.SemaphoreType.DMA((2,))]),
        out_shape=jax.ShapeDtypeStruct((nseq,) + slab, _F32),
        compiler_params=pltpu.CompilerParams(
            dimension_semantics=("arbitrary",), vmem_limit_bytes=VMEM_LIMIT),
        name="sample_attn",
    )(idx, page_table, ck, cv, qt, knt, vnt)


def _rope_tables(pos):
    half = ROPE_DIM // 2
    inv = 1.0 / (ROPE_THETA ** (jnp.arange(half, dtype=_F32) / half))
    ang = pos.astype(_F32)[:, None] * inv[None, :]
    cos, sin = jnp.cos(ang), jnp.sin(ang)
    n = pos.shape[0]
    rest = C_HEAD_DIM - ROPE_DIM
    one, zero, z8 = jnp.ones((n, rest), _F32), jnp.zeros((n, rest), _F32), jnp.zeros((n, half), _F32)
    rep = LANES // C_HEAD_DIM
    cos_t = jnp.tile(jnp.concatenate([cos, cos, one], axis=1), (1, rep))
    sa_t = jnp.tile(jnp.concatenate([-sin, z8, zero], axis=1), (1, rep))
    sb_t = jnp.tile(jnp.concatenate([z8, sin, zero], axis=1), (1, rep))
    return cos_t, sa_t, sb_t


def _to_lane_minor(z, nq, nseq):
    zt = z.reshape(nq, nseq, C_HEADS, C_HEAD_DIM).transpose(1, 2, 3, 0)
    return jnp.pad(zt, ((0, 0), (0, 0), (0, 0), (0, LANES - nq)))


def kernel(x_prompt, x_sample, state_pool, cache_k, cache_v, page_table, g_mix, w_in, g_av,
           w_spatial, b_spatial, w_pool, pool_scale, w_br_a, w_br_b, w_br_c, w_out, g_ffn,
           w_ff1, w_ff2, g_final):
    depth = w_in.shape[0]
    nbp, seq, _ = x_prompt.shape
    nseq, nq, _ = x_sample.shape
    past_len = page_table.shape[1] * PAGE_SIZE
    nblk_past = past_len // C_BLOCK
    rows_s = nseq * nq
    tm_p = 256

    p = {
        'g_mix': g_mix[:, None, :], 'w_in': w_in.astype(_BF16), 'g_av': g_av[:, None, :],
        'w_pool': w_pool.astype(_BF16), 'pool_scale': pool_scale[:, None, :],
        'w_br_a': w_br_a.astype(_BF16), 'w_br_b': w_br_b.astype(_BF16),
        'w_br_c': w_br_c.astype(_BF16), 'w_out': w_out.astype(_BF16),
        'g_ffn': g_ffn[:, None, :], 'w_ff1': w_ff1.astype(_BF16), 'w_ff2': w_ff2.astype(_BF16),
    }
    g_fin = g_final[None, :]

    causal = jnp.tril(jnp.ones((A_CHUNK, A_CHUNK), dtype=bool))
    wsp_p = jnp.where(causal[None, None], w_spatial, 0).astype(_BF16)
    eye = jnp.eye(nseq, dtype=_F32)
    wsp_s = jnp.einsum('lgts,bc->lgtbsc', jnp.where(causal[None, None], w_spatial, 0)[:, :, :nq, :nq],
                       eye).reshape(depth, A_GROUPS, rows_s, rows_s).astype(_BF16)
    bias_rows = jnp.repeat(b_spatial.transpose(0, 2, 1), A_WIDTH // A_GROUPS, axis=2)
    bsp_p = jnp.tile(bias_rows, (1, tm_p // A_CHUNK, 1))
    bsp_s = jnp.repeat(bias_rows[:, :nq], nseq, axis=1)

    rope_p = _rope_tables(jnp.arange(seq))
    rope_s = _rope_tables(past_len + jnp.arange(rows_s) // nseq)

    ck = cache_k.transpose(0, 1, 3, 4, 2)
    cv = cache_v.transpose(0, 1, 3, 4, 2)
    kmt = _kmean(page_table, ck)

    zero_prefix = jnp.zeros((1, POOL_STEPS, B_WIDTH), _F32)
    slab_shape = (depth, nbp, seq // PAGE_SIZE, C_HEADS, C_HEAD_DIM, PAGE_SIZE)
    slabs = (jnp.zeros(slab_shape, _F32), jnp.zeros(slab_shape, _F32))
    xp = x_prompt
    xs = x_sample.transpose(1, 0, 2).reshape(1, rows_s, D_MODEL)
    ks_l, vs_l, pp_l, ps_l, av_l = [], [], [], [], []
    for l in range(depth):
        last = l == depth - 1
        mab, gc, q, k, kt_all, vt_all, tail = _mix_in(
            xp, l, p, wsp_p, bsp_p[l], zero_prefix, *rope_p, tm=tm_p, stride=1, pos0=0,
            slab_shape=slab_shape, slab_prev=slabs)
        slabs = (kt_all, vt_all)
        c = _prompt_attn(q, k, vt_all, l)
        xp = _mix_out(mab, gc, c, xp, l, p, g_fin, tm=tm_p, final=last)
        pp_l.append(tail[:, -1, 1:, :])

        prefix = jnp.pad(state_pool[l].transpose(1, 0, 2), ((1, 0), (0, 0), (0, 0)))
        prefix = prefix.reshape(1, POOL_STEPS * nseq, B_WIDTH)
        mab, gc, q, k, v, vn, tail = _mix_in(xs, l, p, wsp_s, bsp_s[l], prefix, *rope_s,
                                             tm=rows_s, stride=nseq, pos0=past_len)
        qt = _to_lane_minor(q[0], nq, nseq)
        sel = _sample_select(qt, kmt, l, nq=nq, nblk=nblk_past)
        idx = sel[:, :, :C_TOP_K].reshape(-1)
        ct = _sample_attn(idx, page_table, ck, cv, qt, _to_lane_minor(k[0], nq, nseq),
                          _to_lane_minor(v[0], nq, nseq), l, nq=nq)
        c = ct[..., :nq].transpose(3, 0, 1, 2).reshape(1, rows_s, C_WIDTH)
        xs = _mix_out(mab, gc, c, xs, l, p, g_fin, tm=rows_s, final=last)
        ks_l.append(k[0].reshape(nq, nseq, C_HEADS, C_HEAD_DIM).transpose(1, 0, 2, 3))
        vs_l.append(v[0].reshape(nq, nseq, C_HEADS, C_HEAD_DIM).transpose(1, 0, 2, 3))
        ps_l.append(tail[0, 0].reshape(POOL_STEPS, nseq, B_WIDTH)[1:].transpose(1, 0, 2))
        av_l.append(vn[0].reshape(nq, nseq, A_WIDTH).transpose(1, 0, 2))

    y_prompt = xp
    y_sample = xs[0].reshape(nq, nseq, D_MODEL).transpose(1, 0, 2)
    k_prompt = slabs[0].transpose(0, 1, 2, 5, 3, 4)
    v_prompt = slabs[1].transpose(0, 1, 2, 5, 3, 4)
    return (y_prompt, y_sample, k_prompt, v_prompt, jnp.stack(ks_l), jnp.stack(vs_l),
            jnp.stack(pp_l), jnp.stack(ps_l), jnp.stack(av_l))
```

```python
import functools

import jax
import jax.numpy as jnp
from jax import lax
from jax.experimental import pallas as pl
from jax.experimental.pallas import tpu as pltpu

D_MODEL = 1024
A_WIDTH = 512
A_GROUPS = 4
A_CHUNK = 128
B_WIDTH = 512
B_WINDOWS = (2, 4, 8, 16)
B_GROUP_DIM = 128
B_STATE = 15
C_HEADS = 8
C_HEAD_DIM = 64
C_WIDTH = 512
C_BLOCK = 256
C_TOP_K = 3
ROPE_DIM = 16
ROPE_THETA = 500000.0
D_FF = 4096
EPS = 1e-6
PAGE_SIZE = 128
PAGES_PER_BLOCK = C_BLOCK // PAGE_SIZE
POOL_STEPS = B_STATE + 1
LANES = 128
HEADS_PER_TILE = LANES // C_HEAD_DIM
VMEM_LIMIT = 56 * 1024 * 1024

_F32 = jnp.float32
_BF16 = jnp.bfloat16
_NT = (((1,), (1,)), ((), ()))


def _rms(x, g):
    return x * lax.rsqrt(jnp.mean(x * x, axis=-1, keepdims=True) + EPS) * g


def _kmean_sidecar(pt_ref, ck_ref, km_ref, kbuf_ref, sem, *, layer, seq0, pages):
    nsteps = pl.num_programs(0) * pl.num_programs(1)
    s = pl.program_id(0) * pl.num_programs(1) + pl.program_id(1)
    per_seq = pt_ref.shape[1] // pages
    nblk = pages // PAGES_PER_BLOCK

    def copies(step, slot):
        seq = seq0 + step // per_seq
        first = (step % per_seq) * pages
        return [pltpu.make_async_copy(ck_ref.at[layer, pt_ref[seq, first + pg]],
                                      kbuf_ref.at[slot, pg], sem.at[slot])
                for pg in range(pages)]

    slot = s % 2

    @pl.when(s == 0)
    def _():
        for cp in copies(s, slot):
            cp.start()

    @pl.when(s + 1 < nsteps)
    def _():
        for cp in copies(s + 1, 1 - slot):
            cp.start()

    for cp in copies(s, slot):
        cp.wait()
    part = s % per_seq

    @pl.when(part == 0)
    def _():
        km_ref[0] = jnp.zeros(km_ref.shape[1:], _F32)

    lane = lax.broadcasted_iota(jnp.int32, km_ref.shape[1:], 2)
    for jj in range(nblk):
        blk = kbuf_ref[slot, PAGES_PER_BLOCK * jj]
        for pp in range(1, PAGES_PER_BLOCK):
            blk = blk + kbuf_ref[slot, PAGES_PER_BLOCK * jj + pp]
        mean = jnp.sum(blk, axis=-1, keepdims=True) * (1.0 / C_BLOCK)
        km_ref[0] = jnp.where(lane == part * nblk + jj, mean, km_ref[0])


def _mix_in_body(*refs, tm, stride, pos0, carry, slabs, n_alias, km):
    if km:
        pt_ref, refs = refs[0], refs[1:]
    (x_ref, gmix_ref, win_ref, gav_ref, wsp_ref, bsp_ref, prefix_ref, wpool_ref, pscale_ref,
     wbra_ref, wbrb_ref, cos_ref, sa_ref, sb_ref) = refs[:14]
    outs = refs[14 + n_alias:]
    if km:
        ck_ref, outs = outs[0], outs[1:]
        mab_ref, gc_ref, q_ref, k_ref, kt_ref, vt_ref, tail_ref, km_ref, ext_ref, kbuf_ref, sem = outs
        _kmean_sidecar(pt_ref, ck_ref, km_ref, kbuf_ref, sem, **km)
    elif slabs:
        mab_ref, gc_ref, q_ref, k_ref, kt_ref, vt_ref, tail_ref, ext_ref = outs
    else:
        (mab_ref, gc_ref, k_ref, v_ref, vn_ref, tail_ref, qx_ref, kx_ref, vx_ref,
         ext_ref) = outs
    hdr = POOL_STEPS * stride
    i = pl.program_id(1)
    x = x_ref[0]
    hb = _rms(x, gmix_ref[0]).astype(_BF16)

    def seg(lo, hi):
        return jnp.dot(hb, win_ref[0, :, lo:hi], preferred_element_type=_F32)

    u = jax.nn.gelu(seg(0, A_WIDTH), approximate=True)
    vn = _rms(jax.nn.gelu(seg(A_WIDTH, 2 * A_WIDTH), approximate=True), gav_ref[0])
    if not slabs:
        vn_ref[0] = vn
    vnb = vn.astype(_BF16)
    chunks = []
    for c in range(tm // A_CHUNK):
        rows = slice(c * A_CHUNK, (c + 1) * A_CHUNK)
        groups = [jnp.dot(wsp_ref[0, g], vnb[rows, g * LANES:(g + 1) * LANES],
                          preferred_element_type=_F32) for g in range(A_GROUPS)]
        chunks.append(jnp.concatenate(groups, axis=1))
    mixed = chunks[0] if len(chunks) == 1 else jnp.concatenate(chunks, axis=0)
    a_out = u * (mixed + bsp_ref[...])

    xb = seg(2 * A_WIDTH, 2 * A_WIDTH + B_WIDTH)
    if carry:
        @pl.when(i == 0)
        def _():
            ext_ref[0:hdr, :] = prefix_ref[0]

        @pl.when(i > 0)
        def _():
            ext_ref[0:hdr, :] = ext_ref[tm:tm + hdr, :]
    else:
        ext_ref[0:hdr, :] = prefix_ref[0]
    ext_ref[hdr:hdr + tm, :] = xb
    tail_ref[0, 0] = ext_ref[tm:tm + hdr, :]
    row = lax.broadcasted_iota(jnp.int32, (tm, 1), 0)
    pos = pos0 + lax.shift_right_logical(i * tm + row, stride.bit_length() - 1)
    pooled = []
    for gi, w in enumerate(B_WINDOWS):
        cols = slice(gi * B_GROUP_DIM, (gi + 1) * B_GROUP_DIM)
        cur = xb[:, cols]
        s = cur
        for kk in range(1, w):
            s = s + ext_ref[hdr - kk * stride:hdr - kk * stride + tm, cols]
        cnt = jnp.minimum(w, pos + 1).astype(_F32)
        pg = (s / cnt - cur).astype(_BF16)
        pooled.append(jnp.dot(pg, wpool_ref[0, gi], preferred_element_type=_F32))
    b_out = jnp.concatenate(pooled, axis=1) * pscale_ref[0]

    g0 = 2 * A_WIDTH + B_WIDTH + 3 * C_WIDTH
    ga = jax.nn.sigmoid(seg(g0, g0 + D_MODEL))
    gb = jax.nn.sigmoid(seg(g0 + D_MODEL, g0 + 2 * D_MODEL))
    gc_ref[0] = jax.nn.sigmoid(seg(g0 + 2 * D_MODEL, g0 + 3 * D_MODEL))
    mab_ref[0] = (ga * jnp.dot(a_out.astype(_BF16), wbra_ref[0], preferred_element_type=_F32)
                  + gb * jnp.dot(b_out.astype(_BF16), wbrb_ref[0], preferred_element_type=_F32))

    c0 = 2 * A_WIDTH + B_WIDTH
    cosv, sav, sbv = cos_ref[...], sa_ref[...], sb_ref[...]

    def rope(z):
        parts = []
        for cb in range(C_WIDTH // LANES):
            zc = z[:, cb * LANES:(cb + 1) * LANES]
            parts.append(zc * cosv
                         + pltpu.roll(zc, LANES - ROPE_DIM // 2, axis=1) * sav
                         + pltpu.roll(zc, ROPE_DIM // 2, axis=1) * sbv)
        return jnp.concatenate(parts, axis=1)

    q = rope(seg(c0, c0 + C_WIDTH))
    k = rope(seg(c0 + C_WIDTH, c0 + 2 * C_WIDTH))
    v = seg(c0 + 2 * C_WIDTH, c0 + 3 * C_WIDTH)
    k_ref[0] = k
    if slabs:
        q_ref[0] = q
        for pg in range(tm // PAGE_SIZE):
            rows = slice(pg * PAGE_SIZE, (pg + 1) * PAGE_SIZE)
            kt_ref[0, 0, pg] = k[rows, :].T.reshape(C_HEADS, C_HEAD_DIM, PAGE_SIZE)
            vt_ref[0, 0, pg] = v[rows, :].T.reshape(C_HEADS, C_HEAD_DIM, PAGE_SIZE)
    else:
        v_ref[0] = v
        qx_ref[0] = q.T
        kx_ref[0] = k.T
        vx_ref[0] = v.T


def _const_spec(shape, layer):
    nd = len(shape)
    return pl.BlockSpec((1,) + tuple(shape[1:]), lambda *_: (layer,) + (0,) * (nd - 1),
                        pipeline_mode=pl.Buffered(1))


def _km_plumbing(km, nb, nt):
    if km is None:
        return dict(prefetch=(), inputs=(), in_specs=[], out_specs=[], out_shape=[], scratch=[],
                    body=None)
    npages = km['page_table'].shape[1]
    pages = km['nseq'] * npages // (nb * nt)
    per_seq = npages // pages
    slab = km['ck'].shape[2:]
    return dict(
        prefetch=(km['page_table'],), inputs=(km['ck'],),
        in_specs=[pl.BlockSpec(memory_space=pl.ANY)],
        out_specs=[pl.BlockSpec((1,) + slab,
                                lambda b, i, *_: ((b * nt + i) // per_seq, 0, 0, 0))],
        out_shape=[jax.ShapeDtypeStruct((km['nseq'],) + slab, _F32)],
        scratch=[pltpu.VMEM((2, pages) + slab, _F32), pltpu.SemaphoreType.DMA((2,))],
        body=dict(layer=km['layer'], seq0=km['seq0'], pages=pages))


def _mix_in(x, layer, p, wsp, bsp, prefix, cos_t, sa_t, sb_t, *, tm, stride, pos0,
            slab_shape=None, slab_prev=(), km=None):
    nb, s, _ = x.shape
    nt = s // tm
    hdr = POOL_STEPS * stride
    kmp = _km_plumbing(km, nb, nt)
    row3 = lambda width: pl.BlockSpec((1, tm, width), lambda b, i, *_: (b, i, 0))
    tab =pl.BlockSpec((tm, LANES), lambda b, i, *_: (i, 0))
    in_specs = [
        row3(D_MODEL),
        _const_spec(p['g_mix'].shape, layer),
        _const_spec(p['w_in'].shape, layer),
        _const_spec(p['g_av'].shape, layer),
        _const_spec(wsp.shape, layer),
        pl.BlockSpec((tm, A_WIDTH), lambda *_: (0, 0), pipeline_mode=pl.Buffered(1)),
        pl.BlockSpec((1, hdr, B_WIDTH), lambda *_: (0, 0, 0), pipeline_mode=pl.Buffered(1)),
        _const_spec(p['w_pool'].shape, layer),
        _const_spec(p['pool_scale'].shape, layer),
        _const_spec(p['w_br_a'].shape, layer),
        _const_spec(p['w_br_b'].shape, layer),
        tab, tab, tab,
    ] + [pl.BlockSpec(memory_space=pl.ANY)] * len(slab_prev) + kmp['in_specs']
    f = lambda width: jax.ShapeDtypeStruct((nb, s, width), _F32)
    tail_spec = pl.BlockSpec((1, 1, hdr, B_WIDTH), lambda b, i, *_: (b, i, 0, 0))
    tail_shape = jax.ShapeDtypeStruct((nb, nt, hdr, B_WIDTH), _F32)
    aliases = {}
    if slab_shape is not None:
        slab_spec = pl.BlockSpec((1, 1, tm // PAGE_SIZE) + tuple(slab_shape[3:]),
                                 lambda b, i, *_: (layer, b, i, 0, 0, 0))
        slab = jax.ShapeDtypeStruct(slab_shape, _F32)
        out_specs = [row3(D_MODEL), row3(D_MODEL), row3(C_WIDTH), row3(C_WIDTH),
                     slab_spec, slab_spec, tail_spec]
        out_shape = [f(D_MODEL), f(D_MODEL), f(C_WIDTH), f(C_WIDTH), slab, slab, tail_shape]
        first = len(kmp['prefetch']) + 14
        aliases = {first: 4, first + 1: 5}
    else:
        tr_spec = pl.BlockSpec((1, C_WIDTH, tm), lambda b, i, *_: (b, 0, i))
        tr_shape = jax.ShapeDtypeStruct((nb, C_WIDTH, s), _F32)
        out_specs = [row3(D_MODEL), row3(D_MODEL), row3(C_WIDTH), row3(C_WIDTH), row3(A_WIDTH),
                     tail_spec, tr_spec, tr_spec, tr_spec]
        out_shape = [f(D_MODEL), f(D_MODEL), f(C_WIDTH), f(C_WIDTH), f(A_WIDTH), tail_shape,
                     tr_shape, tr_shape, tr_shape]
    body = functools.partial(_mix_in_body, tm=tm, stride=stride, pos0=pos0, carry=nt > 1,
                             slabs=slab_shape is not None, n_alias=len(slab_prev),
                             km=kmp['body'])
    return pl.pallas_call(
        body,
        grid_spec=pltpu.PrefetchScalarGridSpec(
            num_scalar_prefetch=len(kmp['prefetch']), grid=(nb, nt), in_specs=in_specs,
            out_specs=out_specs + kmp['out_specs'],
            scratch_shapes=[pltpu.VMEM((hdr + tm, B_WIDTH), _F32)] + kmp['scratch']),
        out_shape=out_shape + kmp['out_shape'],
        input_output_aliases=aliases,
        compiler_params=pltpu.CompilerParams(
            dimension_semantics=("arbitrary", "arbitrary"), vmem_limit_bytes=VMEM_LIMIT),
        name="mix_in",
    )(*kmp['prefetch'], x, p['g_mix'], p['w_in'], p['g_av'], wsp, bsp, prefix, p['w_pool'],
      p['pool_scale'], p['w_br_a'], p['w_br_b'], cos_t, sa_t, sb_t, *slab_prev, *kmp['inputs'])


def _mix_out_body(*refs, final, ff_chunk, km):
    if km:
        pt_ref, refs = refs[0], refs[1:]
    (mab_ref, gc_ref, c_ref, x_ref, wbrc_ref, wout_ref, gffn_ref, wff1_ref, wff2_ref,
     gfin_ref) = refs[:10]
    if km:
        ck_ref, o_ref, km_ref, kbuf_ref, sem = refs[10:]
        _kmean_sidecar(pt_ref, ck_ref, km_ref, kbuf_ref, sem, **km)
    else:
        o_ref, = refs[10:]
    merged = mab_ref[0] + gc_ref[0] * jnp.dot(c_ref[0].astype(_BF16), wbrc_ref[0],
                                              preferred_element_type=_F32)
    x1 = x_ref[0] + jnp.dot(merged.astype(_BF16), wout_ref[0], preferred_element_type=_F32)
    h2 = _rms(x1, gffn_ref[0]).astype(_BF16)
    ff = None
    for c in range(D_FF // ff_chunk):
        cols = slice(c * ff_chunk, (c + 1) * ff_chunk)
        f = jnp.dot(h2, wff1_ref[0, :, cols], preferred_element_type=_F32)
        f = jnp.square(jnp.maximum(f, 0.0)).astype(_BF16)
        part = jnp.dot(f, wff2_ref[0, cols, :], preferred_element_type=_F32)
        ff = part if ff is None else ff + part
    x2 = x1 + ff
    o_ref[0] = _rms(x2, gfin_ref[...]) if final else x2


def _mix_out(mab, gc, c, x, layer, p, g_final, *, tm, final, km=None):
    nb, s, _ = x.shape
    nt = s // tm
    kmp = _km_plumbing(km, nb, nt)
    row3 = lambda width: pl.BlockSpec((1, tm, width), lambda b, i, *_: (b, i, 0))
    in_specs = [
        row3(D_MODEL), row3(D_MODEL), row3(C_WIDTH), row3(D_MODEL),
        _const_spec(p['w_br_c'].shape, layer),
        _const_spec(p['w_out'].shape, layer),
        _const_spec(p['g_ffn'].shape, layer),
        _const_spec(p['w_ff1'].shape, layer),
        _const_spec(p['w_ff2'].shape, layer),
        pl.BlockSpec((1, D_MODEL), lambda *_: (0, 0), pipeline_mode=pl.Buffered(1)),
    ] + kmp['in_specs']
    body = functools.partial(_mix_out_body, final=final, ff_chunk=1024, km=kmp['body'])
    return pl.pallas_call(
        body,
        grid_spec=pltpu.PrefetchScalarGridSpec(
            num_scalar_prefetch=len(kmp['prefetch']), grid=(nb, nt), in_specs=in_specs,
            out_specs=[row3(D_MODEL)] + kmp['out_specs'], scratch_shapes=kmp['scratch']),
        out_shape=[jax.ShapeDtypeStruct((nb, s, D_MODEL), _F32)] + kmp['out_shape'],
        compiler_params=pltpu.CompilerParams(
            dimension_semantics=("arbitrary", "arbitrary"), vmem_limit_bytes=VMEM_LIMIT),
        name="mix_out",
    )(*kmp['prefetch'], mab, gc, c, x, p['w_br_c'], p['w_out'], p['g_ffn'], p['w_ff1'],
      p['w_ff2'], g_final, *kmp['inputs'])


def _prompt_attn_body(q_ref, k_ref, vt_ref, o_ref, kb_ref, vtb_ref, kmean_ref, *, nblk):
    scale = C_HEAD_DIM ** -0.5 * 1.4426950408889634
    neg = jnp.float32(-jnp.inf)
    for j in range(nblk):
        rows = slice(j * C_BLOCK, (j + 1) * C_BLOCK)
        kf = k_ref[0, rows, :]
        kb_ref[rows, :] = kf.astype(_BF16)
        kmean_ref[j:j + 1, :] = jnp.sum(kf, axis=0, keepdims=True) * (1.0 / C_BLOCK)
    for pg in range(nblk * PAGES_PER_BLOCK):
        vtb_ref[:, pg * PAGE_SIZE:(pg + 1) * PAGE_SIZE] = (
            vt_ref[0, 0, pg].reshape(LANES, PAGE_SIZE).astype(_BF16))

    lane = lax.broadcasted_iota(jnp.int32, (C_BLOCK, LANES), 1)
    blk = lax.broadcasted_iota(jnp.int32, (nblk, C_BLOCK), 0)
    causal = (lax.broadcasted_iota(jnp.int32, (C_BLOCK, C_BLOCK), 0)
              <= lax.broadcasted_iota(jnp.int32, (C_BLOCK, C_BLOCK), 1))
    sub = lax.broadcasted_iota(jnp.int32, (LANES, C_BLOCK), 0)
    for qi in range(nblk):
        qf = q_ref[0, qi * C_BLOCK:(qi + 1) * C_BLOCK, :]
        heads = []
        for hh in range(HEADS_PER_TILE):
            qh = jnp.where((lane >= hh * C_HEAD_DIM) & (lane < (hh + 1) * C_HEAD_DIM), qf, 0.0)
            qhb = (qh * scale).astype(_BF16)
            bias = None
            if qi > C_TOP_K:
                gate = lax.dot_general(kmean_ref[...], qh, _NT, precision=lax.Precision.HIGHEST,
                                       preferred_element_type=_F32)
                gate = jnp.where(blk < qi, gate, neg)
                rank = jnp.zeros(gate.shape, jnp.int32)
                for j2 in range(qi):
                    other = gate[j2:j2 + 1, :]
                    ahead = (other > gate) | ((other == gate) & (j2 < blk))
                    rank = rank + ahead.astype(jnp.int32)
                bias = jnp.where(rank < C_TOP_K, 0.0, neg)

            nk = (qi + 1) * C_BLOCK
            st = lax.dot_general(kb_ref[0:nk, :], qhb, _NT, preferred_element_type=_F32)
            pieces = []
            for j in range(qi):
                sj = st[j * C_BLOCK:(j + 1) * C_BLOCK, :]
                pieces.append(sj if bias is None else sj + bias[j:j + 1, :])
            pieces.append(jnp.where(causal, st[qi * C_BLOCK:nk, :], neg))
            m = jnp.max(pieces[0], axis=0, keepdims=True)
            for sj in pieces[1:]:
                m = jnp.maximum(m, jnp.max(sj, axis=0, keepdims=True))
            probs = [jnp.exp2(sj - m) for sj in pieces]
            den = jnp.sum(probs[0], axis=0, keepdims=True)
            for pj in probs[1:]:
                den = den + jnp.sum(pj, axis=0, keepdims=True)
            pb = probs[0] if len(probs) == 1 else jnp.concatenate(probs, axis=0)
            acc = jnp.dot(vtb_ref[:, 0:nk], pb.astype(_BF16), preferred_element_type=_F32)
            heads.append(acc / den)
        out_t = heads[0]
        for hh in range(1, HEADS_PER_TILE):
            out_t = jnp.where(sub < hh * C_HEAD_DIM, out_t, heads[hh])
        o_ref[0, qi * C_BLOCK:(qi + 1) * C_BLOCK, :] = out_t.T


def _prompt_attn(q, k, vt_all, layer):
    nb, s, _ = q.shape
    nblk = s // C_BLOCK
    npair = C_WIDTH // LANES
    qspec = pl.BlockSpec((1, s, LANES), lambda b, hp: (b, 0, hp))
    vtspec = pl.BlockSpec((1, 1, s // PAGE_SIZE, HEADS_PER_TILE, C_HEAD_DIM, PAGE_SIZE),
                          lambda b, hp: (layer, b, 0, hp, 0, 0))
    return pl.pallas_call(
        functools.partial(_prompt_attn_body, nblk=nblk),
        grid=(nb, npair), in_specs=[qspec, qspec, vtspec], out_specs=qspec,
        out_shape=jax.ShapeDtypeStruct((nb, s, C_WIDTH), _F32),
        scratch_shapes=[pltpu.VMEM((s, LANES), _BF16),
                        pltpu.VMEM((LANES, s), _BF16),
                        pltpu.VMEM((nblk, LANES), _F32)],
        compiler_params=pltpu.CompilerParams(
            dimension_semantics=("arbitrary", "arbitrary"), vmem_limit_bytes=VMEM_LIMIT),
        name="prompt_attn",
    )(q, k, vt_all)


_SEL_SEQS = 4


def _sample_select_body(qt_ref, km_ref, o_ref, *, nq, nseq, nblk):
    lane = lax.broadcasted_iota(jnp.int32, (1, LANES), 1)
    lane_f = lane.astype(_F32)
    lane_q = lax.broadcasted_iota(jnp.int32, (C_HEAD_DIM, LANES), 1)
    neg = jnp.float32(-jnp.inf)
    rows = []
    for sq in range(_SEL_SEQS):
        b = pl.program_id(0) * _SEL_SEQS + sq
        for h in range(C_HEADS):
            km = km_ref[sq, h]
            qh = qt_ref[h * C_HEAD_DIM:(h + 1) * C_HEAD_DIM, :]
            for t in range(nq):
                qcol = jnp.sum(jnp.where(lane_q == t * nseq + b, qh, 0.0), axis=1, keepdims=True)
                rows.append(jnp.sum(km * qcol, axis=0, keepdims=True))
    gate = jnp.where(lane < nblk, jnp.concatenate(rows, axis=0), neg)
    out = jnp.zeros(gate.shape, _F32)
    for r in range(C_TOP_K):
        best = jnp.max(gate, axis=1, keepdims=True)
        idx = jnp.min(jnp.where(gate == best, lane_f, float(LANES)), axis=1,
                      keepdims=True)
        gate = jnp.where(lane_f == idx, neg, gate)
        out = jnp.where(lane == r, idx, out)
    o_ref[...] = out.astype(jnp.int32).reshape(o_ref.shape)


def _sample_select(qt, kmt, *, nq, nblk):
    nseq = kmt.shape[0]
    return pl.pallas_call(
        functools.partial(_sample_select_body, nq=nq, nseq=nseq, nblk=nblk),
        grid=(nseq // _SEL_SEQS,),
        in_specs=[pl.BlockSpec(qt.shape, lambda b: (0, 0)),
                  pl.BlockSpec((_SEL_SEQS,) + kmt.shape[1:], lambda b: (b, 0, 0, 0))],
        out_specs=pl.BlockSpec((_SEL_SEQS, C_HEADS * nq, LANES), lambda b: (b, 0, 0)),
        out_shape=jax.ShapeDtypeStruct((nseq, C_HEADS * nq, LANES), jnp.int32),
        compiler_params=pltpu.CompilerParams(dimension_semantics=("arbitrary",)),
        name="sample_select",
    )(qt, kmt)


def _sample_attn_body(idx_ref, pt_ref, ck_ref, cv_ref, qt_ref, kn_ref, vn_ref, o_ref,
                      kbuf_ref, vbuf_ref, sem, *, layer, nq):
    b = pl.program_id(0)
    nseq = pl.num_programs(0)
    nslab = C_TOP_K * PAGES_PER_BLOCK
    scale = C_HEAD_DIM ** -0.5

    def for_each_copy(bb, slot, fn):
        def per_head(h, _):
            for t in range(nq):
                for r in range(C_TOP_K):
                    blk = idx_ref[((bb * C_HEADS + h) * nq + t) * C_TOP_K + r]
                    for pg in range(PAGES_PER_BLOCK):
                        phys = pt_ref[bb, blk * PAGES_PER_BLOCK + pg]
                        dst = (slot, t * C_HEADS + h, r * PAGES_PER_BLOCK + pg)
                        fn(pltpu.make_async_copy(ck_ref.at[layer, phys, h], kbuf_ref.at[dst],
                                                 sem.at[slot]))
                        fn(pltpu.make_async_copy(cv_ref.at[layer, phys, h], vbuf_ref.at[dst],
                                                 sem.at[slot]))
            return 0
        lax.fori_loop(0, C_HEADS, per_head, 0)

    slot = b % 2

    @pl.when(b == 0)
    def _():
        for_each_copy(b, slot, lambda cp: cp.start())

    @pl.when(b + 1 < nseq)
    def _():
        for_each_copy(b + 1, 1 - slot, lambda cp: cp.start())

    for_each_copy(b, slot, lambda cp: cp.wait())

    lane = lax.broadcasted_iota(jnp.int32, (1, LANES), 1)
    lane_o = lax.broadcasted_iota(jnp.int32, (C_HEAD_DIM, LANES), 1)
    mine = (lane & (nseq - 1)) == b
    neg = jnp.float32(-jnp.inf)

    @pl.when(b == 0)
    def _():
        o_ref[...] = jnp.zeros(o_ref.shape, _F32)

    def per_head(h, _):
        hrows = pl.ds(pl.multiple_of(h * C_HEAD_DIM, C_HEAD_DIM), C_HEAD_DIM)
        qt = qt_ref[hrows, :]
        kn = kn_ref[hrows, :]
        vn = vn_ref[hrows, :]
        out = o_ref[hrows, :]
        for t in range(nq):
            col_t = t * nseq + b
            qcol = jnp.sum(jnp.where(lane_o == col_t, qt, 0.0), axis=1, keepdims=True)
            s_own = jnp.sum(kn * qcol, axis=0, keepdims=True) * scale
            s_own = jnp.where(mine & (lane <= col_t), s_own, neg)
            m = jnp.max(s_own, axis=1, keepdims=True)
            s_sel = []
            for sl in range(nslab):
                s = jnp.sum(kbuf_ref[slot, t * C_HEADS + h, sl] * qcol, axis=0,
                            keepdims=True) * scale
                s_sel.append(s)
                m = jnp.maximum(m, jnp.max(s, axis=1, keepdims=True))
            p_own = jnp.exp(s_own - m)
            den = jnp.sum(p_own, axis=1, keepdims=True)
            acc = vn * p_own
            for sl in range(nslab):
                p = jnp.exp(s_sel[sl] - m)
                den = den + jnp.sum(p, axis=1, keepdims=True)
                acc = acc + vbuf_ref[slot, t * C_HEADS + h, sl] * p
            col = jnp.sum(acc, axis=1, keepdims=True) / den
            out = jnp.where(lane_o == col_t, col, out)
        o_ref[hrows, :] = out
        return 0

    lax.fori_loop(0, C_HEADS, per_head, 0)


def _sample_attn(idx, page_table, ck, cv, qt, knt, vnt, layer, *, nq):
    nseq = page_table.shape[0]
    assert nseq & (nseq - 1) == 0 and qt.shape == (C_WIDTH, nq * nseq) and nq * nseq == LANES
    nslab = C_TOP_K * PAGES_PER_BLOCK
    bspec = pl.BlockSpec(qt.shape, lambda b, idx, pt: (0, 0))
    any_spec = pl.BlockSpec(memory_space=pl.ANY)
    buf = pltpu.VMEM((2, nq * C_HEADS, nslab, C_HEAD_DIM, PAGE_SIZE), _F32)
    return pl.pallas_call(
        functools.partial(_sample_attn_body, layer=layer, nq=nq),
        grid_spec=pltpu.PrefetchScalarGridSpec(
            num_scalar_prefetch=2, grid=(nseq,),
            in_specs=[any_spec, any_spec, bspec, bspec, bspec],
            out_specs=bspec,
            scratch_shapes=[buf, buf, pltpu.SemaphoreType.DMA((2,))]),
        out_shape=jax.ShapeDtypeStruct(qt.shape, _F32),
        compiler_params=pltpu.CompilerParams(
            dimension_semantics=("arbitrary",), vmem_limit_bytes=VMEM_LIMIT),
        name="sample_attn",
    )(idx, page_table, ck, cv, qt, knt, vnt)


def _rope_tables(pos):
    half = ROPE_DIM // 2
    inv = 1.0 / (ROPE_THETA ** (jnp.arange(half, dtype=_F32) / half))
    ang = pos.astype(_F32)[:, None] * inv[None, :]
    cos, sin = jnp.cos(ang), jnp.sin(ang)
    n = pos.shape[0]
    rest = C_HEAD_DIM - ROPE_DIM
    one, zero, z8 = jnp.ones((n, rest), _F32), jnp.zeros((n, rest), _F32), jnp.zeros((n, half), _F32)
    rep = LANES // C_HEAD_DIM
    cos_t = jnp.tile(jnp.concatenate([cos, cos, one], axis=1), (1, rep))
    sa_t = jnp.tile(jnp.concatenate([-sin, z8, zero], axis=1), (1, rep))
    sb_t = jnp.tile(jnp.concatenate([z8, sin, zero], axis=1), (1, rep))
    return cos_t, sa_t, sb_t


def kernel(x_prompt, x_sample, state_pool, cache_k, cache_v, page_table, g_mix, w_in, g_av,
           w_spatial, b_spatial, w_pool, pool_scale, w_br_a, w_br_b, w_br_c, w_out, g_ffn,
           w_ff1, w_ff2, g_final):
    depth = w_in.shape[0]
    nbp, seq, _ = x_prompt.shape
    nseq, nq, _ = x_sample.shape
    past_len = page_table.shape[1] * PAGE_SIZE
    nblk_past = past_len // C_BLOCK
    rows_s = nseq * nq
    tm_p = 256

    p = {
        'g_mix': g_mix[:, None, :], 'w_in': w_in.astype(_BF16), 'g_av': g_av[:, None, :],
        'w_pool': w_pool.astype(_BF16), 'pool_scale': pool_scale[:, None, :],
        'w_br_a': w_br_a.astype(_BF16), 'w_br_b': w_br_b.astype(_BF16),
        'w_br_c': w_br_c.astype(_BF16), 'w_out': w_out.astype(_BF16),
        'g_ffn': g_ffn[:, None, :], 'w_ff1': w_ff1.astype(_BF16), 'w_ff2': w_ff2.astype(_BF16),
    }
    g_fin = g_final[None, :]

    causal = jnp.tril(jnp.ones((A_CHUNK, A_CHUNK), dtype=bool))
    wsp_p = jnp.where(causal[None, None], w_spatial, 0).astype(_BF16)
    eye = jnp.eye(nseq, dtype=_F32)
    wsp_s = jnp.einsum('lgts,bc->lgtbsc', jnp.where(causal[None, None], w_spatial, 0)[:, :, :nq, :nq],
                       eye).reshape(depth, A_GROUPS, rows_s, rows_s).astype(_BF16)
    bias_rows = jnp.repeat(b_spatial.transpose(0, 2, 1), A_WIDTH // A_GROUPS, axis=2)
    bsp_p = jnp.tile(bias_rows, (1, tm_p // A_CHUNK, 1))
    bsp_s = jnp.repeat(bias_rows[:, :nq], nseq, axis=1)

    rope_p = _rope_tables(jnp.arange(seq))
    rope_s = _rope_tables(past_len + jnp.arange(rows_s) // nseq)

    ck = cache_k.transpose(0, 1, 3, 4, 2)
    cv = cache_v.transpose(0, 1, 3, 4, 2)
    half = nseq // 2

    zero_prefix = jnp.zeros((1, POOL_STEPS, B_WIDTH), _F32)
    slab_shape = (depth, nbp, seq // PAGE_SIZE, C_HEADS, C_HEAD_DIM, PAGE_SIZE)
    slabs = (jnp.zeros(slab_shape, _F32), jnp.zeros(slab_shape, _F32))
    xp = x_prompt
    xs = x_sample.transpose(1, 0, 2).reshape(1, rows_s, D_MODEL)
    ks_l, vs_l, pp_l, ps_l, av_l = [], [], [], [], []
    for l in range(depth):
        last = l == depth - 1
        km = dict(page_table=page_table, ck=ck, layer=l, nseq=half)
        mab, gc, q, k, kt_all, vt_all, tail, km_lo = _mix_in(
            xp, l, p, wsp_p, bsp_p[l], zero_prefix, *rope_p, tm=tm_p, stride=1, pos0=0,
            slab_shape=slab_shape, slab_prev=slabs, km=dict(km, seq0=0))
        slabs = (kt_all, vt_all)
        c = _prompt_attn(q, k, vt_all, l)
        xp, km_hi = _mix_out(mab, gc, c, xp, l, p, g_fin, tm=tm_p, final=last,
                             km=dict(km, seq0=half))
        kmt = jnp.concatenate([km_lo, km_hi], axis=0)
        pp_l.append(tail[:, -1, 1:, :])

        prefix = jnp.pad(state_pool[l].transpose(1, 0, 2), ((1, 0), (0, 0), (0, 0)))
        prefix = prefix.reshape(1, POOL_STEPS * nseq, B_WIDTH)
        mab, gc, k, v, vn, tail, qx, kx, vx = _mix_in(
            xs, l, p, wsp_s, bsp_s[l], prefix, *rope_s, tm=rows_s, stride=nseq, pos0=past_len)
        sel = _sample_select(qx[0], kmt, nq=nq, nblk=nblk_past)
        idx = sel[:, :, :C_TOP_K].reshape(-1)
        ct = _sample_attn(idx, page_table, ck, cv, qx[0], kx[0], vx[0], l, nq=nq)
        xs, = _mix_out(mab, gc, ct.T[None], xs, l, p, g_fin, tm=rows_s, final=last)
        ks_l.append(k[0].reshape(nq, nseq, C_HEADS, C_HEAD_DIM).transpose(1, 0, 2, 3))
        vs_l.append(v[0].reshape(nq, nseq, C_HEADS, C_HEAD_DIM).transpose(1, 0, 2, 3))
        ps_l.append(tail[0, 0].reshape(POOL_STEPS, nseq, B_WIDTH)[1:].transpose(1, 0, 2))
        av_l.append(vn[0].reshape(nq, nseq, A_WIDTH).transpose(1, 0, 2))

    y_prompt = xp
    y_sample = xs[0].reshape(nq, nseq, D_MODEL).transpose(1, 0, 2)
    k_prompt = slabs[0].transpose(0, 1, 2, 5, 3, 4)
    v_prompt = slabs[1].transpose(0, 1, 2, 5, 3, 4)
    return (y_prompt, y_sample, k_prompt, v_prompt, jnp.stack(ks_l), jnp.stack(vs_l),
            jnp.stack(pp_l), jnp.stack(ps_l), jnp.stack(av_l))
```
